```python
import math
import jax, jax.numpy as jnp
from jax import lax
import numpy as np

D_MODEL = 4096
BATCH = 1
SEQ = 16384
DEPTH = 1

GRID_W = 64
CTX_LEN = 256
HEAD_DIM = 128
N_HEADS = 16
N_KV_HEADS = 4
KV_GROUP = N_HEADS // N_KV_HEADS
ROPE_THETA = 10000.0
ROPE_AXIS_DIM = HEAD_DIM // 2
Q_BLOCK = 128
CONV_W = 2048
CONV_KSIZE = 31
N_EXPERTS = 16
EXPERT_FF = 2048
CAPACITY_FACTOR = 2
N_MOD = 6
Q_W = N_HEADS * HEAD_DIM
KV_W = N_KV_HEADS * HEAD_DIM
IN_W = Q_W + 2 * KV_W + 2 * CONV_W + 2 * D_MODEL
EPS = 1e-6

kernel_name = 'hybrid_gqa_conformer_ec_moe_block'


def rms_norm(x, g):
    xf = x.astype(jnp.float32)
    y = xf * lax.rsqrt(jnp.mean(xf * xf, axis=-1, keepdims=True) + EPS)
    return (y * g.astype(jnp.float32)).astype(x.dtype)


def modulate(h, shift, scale):
    return h * (1 + scale) + shift


def axial_rope_tables(n_tokens, dtype):
    n_rows = n_tokens // GRID_W
    row = jnp.repeat(jnp.arange(n_rows, dtype=jnp.int32), GRID_W)
    col = jnp.tile(jnp.arange(GRID_W, dtype=jnp.int32), n_rows)
    pos = jnp.stack([row, col], axis=-1).astype(jnp.float32)
    inv_freq = ROPE_THETA ** (-jnp.arange(0, ROPE_AXIS_DIM, 2, dtype=jnp.float32) / ROPE_AXIS_DIM)
    ang = pos[:, :, None] * inv_freq
    return jnp.cos(ang).astype(dtype), jnp.sin(ang).astype(dtype)


def apply_rope(x, cos, sin):
    B, T, H, D = x.shape
    xr = x.reshape(B, T, H, 2, 2, ROPE_AXIS_DIM // 2)
    x1, x2 = xr[..., 0, :], xr[..., 1, :]
    cs, sn = cos[None, :, None], sin[None, :, None]
    out = jnp.stack([x1 * cs - x2 * sn, x2 * cs + x1 * sn], axis=-2)
    return out.reshape(B, T, H, D)


def split_heads(t, n_heads):
    return t.reshape(t.shape[0], t.shape[1], n_heads, HEAD_DIM)


def split_projection(p, q_g, k_g):
    o1 = Q_W
    o2 = o1 + KV_W
    o3 = o2 + KV_W
    o4 = o3 + 2 * CONV_W
    q = rms_norm(split_heads(p[..., :o1], N_HEADS), q_g)
    k = rms_norm(split_heads(p[..., o1:o2], N_KV_HEADS), k_g)
    v = split_heads(p[..., o2:o3], N_KV_HEADS)
    return q, k, v, p[..., o3:o4], p[..., o4:]


def project_kv(h, w_in, b_in, k_g):
    p = h @ w_in[:, Q_W:Q_W + 2 * KV_W] + b_in[Q_W:Q_W + 2 * KV_W]
    k = rms_norm(split_heads(p[..., :KV_W], N_KV_HEADS), k_g)
    v = split_heads(p[..., KV_W:], N_KV_HEADS)
    return k, v


def gqa_attend(q, k, v):
    B, Tq = q.shape[0], q.shape[1]
    nb = Tq // Q_BLOCK
    qb = q.reshape(B, nb, Q_BLOCK, N_KV_HEADS, KV_GROUP, HEAD_DIM).transpose(1, 0, 2, 3, 4, 5)
    scale = HEAD_DIM ** -0.5

    def one_block(q_blk):
        s = jnp.einsum('bqkgd,bnkd->bkgqn', q_blk, k).astype(jnp.float32) * scale
        p = jax.nn.softmax(s, axis=-1).astype(v.dtype)
        return jnp.einsum('bkgqn,bnkd->bqkgd', p, v)

    o = lax.map(one_block, qb)
    return o.transpose(1, 0, 2, 3, 4, 5).reshape(B, Tq, Q_W)


def conformer_conv(u, dw_w, dw_b, ln_g, ln_b, w_conv_o):
    a, g = jnp.split(u, 2, axis=-1)
    y = a * jax.nn.sigmoid(g)
    y = lax.conv_general_dilated(
        y, dw_w[:, None, :].astype(y.dtype), window_strides=(1,),
        padding=[(CONV_KSIZE // 2, CONV_KSIZE // 2)],
        dimension_numbers=('NWC', 'WIO', 'NWC'), feature_group_count=CONV_W) + dw_b
    yf = y.astype(jnp.float32)
    mu = jnp.mean(yf, axis=-1, keepdims=True)
    var = jnp.mean(jnp.square(yf - mu), axis=-1, keepdims=True)
    yf = (yf - mu) * lax.rsqrt(var + EPS) * ln_g.astype(jnp.float32) + ln_b.astype(jnp.float32)
    y = jax.nn.silu(yf).astype(u.dtype)
    return y @ w_conv_o


def mixer(q, k, v, u, gates, w_attn_o, dw_w, dw_b, ln_g, ln_b, w_conv_o, w_out):
    branch_a = gqa_attend(q, k, v) @ w_attn_o
    branch_b = conformer_conv(u, dw_w, dw_b, ln_g, ln_b, w_conv_o)
    ga, gb = jnp.split(gates, 2, axis=-1)
    merged = jax.nn.sigmoid(ga) * branch_a + jax.nn.sigmoid(gb) * branch_b
    return merged @ w_out


def expert_choice_ffn(h, w_router, w_gate, w_up, w_down):
    B, T, D = h.shape
    cap = CAPACITY_FACTOR * T // N_EXPERTS
    aff = jax.nn.softmax((h @ w_router).astype(jnp.float32), axis=-1)
    vals, idx = lax.top_k(jnp.swapaxes(aff, 1, 2), cap)
    xg = jax.vmap(lambda hb, ib: hb[ib])(h, idx)
    hid = jax.nn.silu(jnp.einsum('becd,edf->becf', xg, w_gate)) * jnp.einsum('becd,edf->becf', xg, w_up)
    y = jnp.einsum('becf,efd->becd', hid, w_down) * vals[..., None].astype(h.dtype)
    return jax.vmap(lambda yb, ib: jnp.zeros((T, D), yb.dtype).at[ib.reshape(-1)].add(yb.reshape(-1, D)))(y, idx)


def setup_inputs(seed: int = 0) -> dict:
    key = jax.random.key(seed)
    ks = jax.random.split(key, 24)

    def nrm(k, shape, scale):
        return jax.random.normal(k, shape, jnp.float32) * scale

    return {
        'x': nrm(ks[0], (BATCH, SEQ, D_MODEL), 1.0),
        'c': nrm(ks[1], (BATCH, D_MODEL), 1.0),
        'ctx': nrm(ks[2], (BATCH, CTX_LEN, D_MODEL), 1.0),
        'c_ctx': nrm(ks[3], (D_MODEL,), 1.0),
        'w_ada': nrm(ks[4], (DEPTH, D_MODEL, N_MOD * D_MODEL), 0.5 * D_MODEL ** -0.5),
        'b_ada': nrm(ks[5], (DEPTH, N_MOD * D_MODEL), 0.02),
        'norm1_g': 1.0 + nrm(ks[6], (DEPTH, D_MODEL), 0.02),
        'w_in': nrm(ks[7], (DEPTH, D_MODEL, IN_W), D_MODEL ** -0.5),
        'b_in': nrm(ks[8], (DEPTH, IN_W), 0.02),
        'q_norm_g': 1.0 + nrm(ks[9], (DEPTH, HEAD_DIM), 0.02),
        'k_norm_g': 1.0 + nrm(ks[10], (DEPTH, HEAD_DIM), 0.02),
        'w_attn_o': nrm(ks[11], (DEPTH, Q_W, D_MODEL), Q_W ** -0.5),
        'dw_w': nrm(ks[12], (DEPTH, CONV_KSIZE, CONV_W), CONV_KSIZE ** -0.5),
        'dw_b': nrm(ks[13], (DEPTH, CONV_W), 0.02),
        'conv_ln_g': 1.0 + nrm(ks[14], (DEPTH, CONV_W), 0.02),
        'conv_ln_b': nrm(ks[15], (DEPTH, CONV_W), 0.02),
        'w_conv_o': nrm(ks[16], (DEPTH, CONV_W, D_MODEL), CONV_W ** -0.5),
        'w_out': nrm(ks[17], (DEPTH, D_MODEL, D_MODEL), D_MODEL ** -0.5),
        'norm2_g': 1.0 + nrm(ks[18], (DEPTH, D_MODEL), 0.02),
        'w_router': nrm(ks[19], (DEPTH, D_MODEL, N_EXPERTS), D_MODEL ** -0.5),
        'w_gate': nrm(ks[20], (DEPTH, N_EXPERTS, D_MODEL, EXPERT_FF), D_MODEL ** -0.5),
        'w_up': nrm(ks[21], (DEPTH, N_EXPERTS, D_MODEL, EXPERT_FF), D_MODEL ** -0.5),
        'w_down': nrm(ks[22], (DEPTH, N_EXPERTS, EXPERT_FF, D_MODEL), EXPERT_FF ** -0.5),
        'final_g': 1.0 + nrm(ks[23], (D_MODEL,), 0.02),
    }


def reference(x, c, ctx, c_ctx, w_ada, b_ada, norm1_g, w_in, b_in, q_norm_g, k_norm_g, w_attn_o,
              dw_w, dw_b, conv_ln_g, conv_ln_b, w_conv_o, w_out, norm2_g, w_router, w_gate, w_up,
              w_down, final_g):
    n_tokens = x.shape[1]
    cos, sin = axial_rope_tables(n_tokens, x.dtype)
    x_lat, x_ctx = x, ctx
    for l in range(DEPTH):
        last = l == DEPTH - 1
        mod_lat = (jax.nn.silu(c) @ w_ada[l] + b_ada[l])[:, None, :]
        mod_ctx = (jax.nn.silu(c_ctx) @ w_ada[l] + b_ada[l])[None, None, :]
        sh1, sc1, g1, sh2, sc2, g2 = jnp.split(mod_lat, N_MOD, axis=-1)
        csh1, csc1, cg1, csh2, csc2, cg2 = jnp.split(mod_ctx, N_MOD, axis=-1)

        h_lat = modulate(rms_norm(x_lat, norm1_g[l]), sh1, sc1)
        h_ctx = modulate(rms_norm(x_ctx, norm1_g[l]), csh1, csc1)

        if last:
            k_ctx, v_ctx = project_kv(h_ctx, w_in[l], b_in[l], k_norm_g[l])
        else:
            q_ctx, k_ctx, v_ctx, u_ctx, gates_ctx = split_projection(
                h_ctx @ w_in[l] + b_in[l], q_norm_g[l], k_norm_g[l])
            ctx_mix = mixer(q_ctx, k_ctx, v_ctx, u_ctx, gates_ctx, w_attn_o[l], dw_w[l], dw_b[l],
                            conv_ln_g[l], conv_ln_b[l], w_conv_o[l], w_out[l])
            x_ctx = x_ctx + cg1 * ctx_mix
            h2_ctx = modulate(rms_norm(x_ctx, norm2_g[l]), csh2, csc2)
            x_ctx = x_ctx + cg2 * expert_choice_ffn(h2_ctx, w_router[l], w_gate[l], w_up[l], w_down[l])

        q_lat, k_lat, v_lat, u_lat, gates_lat = split_projection(
            h_lat @ w_in[l] + b_in[l], q_norm_g[l], k_norm_g[l])
        q_lat = apply_rope(q_lat, cos, sin)
        k_lat = apply_rope(k_lat, cos, sin)
        k_all = jnp.concatenate([k_ctx, k_lat], axis=1)
        v_all = jnp.concatenate([v_ctx, v_lat], axis=1)
        lat_mix = mixer(q_lat, k_all, v_all, u_lat, gates_lat, w_attn_o[l], dw_w[l], dw_b[l],
                        conv_ln_g[l], conv_ln_b[l], w_conv_o[l], w_out[l])
        x_lat = x_lat + g1 * lat_mix

        h2_lat = modulate(rms_norm(x_lat, norm2_g[l]), sh2, sc2)
        x_lat = x_lat + g2 * expert_choice_ffn(h2_lat, w_router[l], w_gate[l], w_up[l], w_down[l])

    return rms_norm(x_lat, final_g)
```

```python
import functools

import jax
import jax.numpy as jnp
from jax import lax
from jax.experimental import pallas as pl
from jax.experimental.pallas import tpu as pltpu

F32 = jnp.float32
BF16 = jnp.bfloat16

HEAD_DIM = 128
GRID_W = 64
ROPE_THETA = 10000.0
ROPE_AXIS_DIM = HEAD_DIM // 2
EPS = 1e-6
CAPACITY_FACTOR = 2
N_MOD = 6
HALO = 16
LANE = 128
MIB = 1024 * 1024


def _tile(n, pref, mult=LANE):
    if n <= pref:
        return n
    t = (pref // mult) * mult
    while t >= mult:
        if n % t == 0:
            return t
        t -= mult
    return n


def _params(sem, vmem_mib):
    return pltpu.CompilerParams(dimension_semantics=sem, vmem_limit_bytes=vmem_mib * MIB)


def _ada_kernel(s_ref, w_ref, b_ref, o_ref):
    k = pl.program_id(1)

    @pl.when(k == 0)
    def _():
        o_ref[...] = jnp.broadcast_to(b_ref[...], o_ref.shape)

    s = s_ref[...]
    s = s * jax.nn.sigmoid(s)
    w = w_ref[...]
    r0 = jnp.sum(s[:, 0:1] * w, axis=0, keepdims=True)
    r1 = jnp.sum(s[:, 1:2] * w, axis=0, keepdims=True)
    o_ref[0:1, :] += r0
    o_ref[1:2, :] += r1


def ada_modulation(cvecs_t, w_ada, b_ada):
    d, n = w_ada.shape
    tk = _tile(d, 256, 8)
    tn = _tile(n, 2048)
    return pl.pallas_call(
        _ada_kernel,
        grid=(n // tn, d // tk),
        in_specs=[
            pl.BlockSpec((tk, 2), lambda j, k: (k, 0)),
            pl.BlockSpec((tk, tn), lambda j, k: (k, j)),
            pl.BlockSpec((1, tn), lambda j, k: (0, j)),
        ],
        out_specs=pl.BlockSpec((2, tn), lambda j, k: (0, j)),
        out_shape=jax.ShapeDtypeStruct((2, n), F32),
        compiler_params=_params(("parallel", "arbitrary"), 32),
        name="ada_modulation",
    )(cvecs_t, w_ada, b_ada)


def _normmod_kernel(x_ref, g_ref, sc_ref, sh_ref, o_ref):
    x = x_ref[...].astype(F32)
    ms = jnp.mean(x * x, axis=-1, keepdims=True)
    y = x * lax.rsqrt(ms + EPS) * g_ref[...]
    o_ref[...] = (y * (1.0 + sc_ref[...]) + sh_ref[...]).astype(o_ref.dtype)


def norm_modulate(x, g, scale, shift, out_dtype=BF16):
    t, d = x.shape
    tm = _tile(t, 256, 8)
    vec = pl.BlockSpec((1, d), lambda i: (0, 0))
    return pl.pallas_call(
        _normmod_kernel,
        grid=(t // tm,),
        in_specs=[pl.BlockSpec((tm, d), lambda i: (i, 0)), vec, vec, vec],
        out_specs=pl.BlockSpec((tm, d), lambda i: (i, 0)),
        out_shape=jax.ShapeDtypeStruct((t, d), out_dtype),
        compiler_params=_params(("parallel",), 40),
        name="norm_modulate",
    )(x, g, scale, shift)


def _mm_bias_kernel(a_ref, w_ref, b_ref, o_ref):
    acc = jnp.dot(a_ref[...], w_ref[...], preferred_element_type=F32)
    o_ref[...] = (acc + b_ref[...]).astype(o_ref.dtype)


def matmul_bias(a, w, b, out_dtype=BF16, tm_pref=1024, tn_pref=1024):
    m, k = a.shape
    n = w.shape[1]
    tm = _tile(m, tm_pref, 8)
    tn = _tile(n, tn_pref)
    return pl.pallas_call(
        _mm_bias_kernel,
        grid=(n // tn, m // tm),
        in_specs=[
            pl.BlockSpec((tm, k), lambda j, i: (i, 0)),
            pl.BlockSpec((k, tn), lambda j, i: (0, j)),
            pl.BlockSpec((1, tn), lambda j, i: (0, j)),
        ],
        out_specs=pl.BlockSpec((tm, tn), lambda j, i: (i, j)),
        out_shape=jax.ShapeDtypeStruct((m, n), out_dtype),
        compiler_params=_params(("parallel", "parallel"), 56),
        name="matmul_bias",
    )(a, w, b)


def _headnorm_kernel(x_ref, g_ref, cos_ref, sin_ref, o_ref, *, n_heads, rope, scale):
    g = g_ref[...]
    if rope:
        cosf = cos_ref[...]
        sins = sin_ref[...]
        lane = lax.broadcasted_iota(jnp.int32, cosf.shape, 1)
        first_half = (lane % (ROPE_AXIS_DIM)) < (ROPE_AXIS_DIM // 2)
    for h in range(n_heads):
        x = x_ref[:, h * HEAD_DIM:(h + 1) * HEAD_DIM].astype(F32)
        ms = jnp.mean(x * x, axis=-1, keepdims=True)
        y = x * lax.rsqrt(ms + EPS) * g
        if rope:
            up = pltpu.roll(y, HEAD_DIM - ROPE_AXIS_DIM // 2, axis=1)
            dn = pltpu.roll(y, ROPE_AXIS_DIM // 2, axis=1)
            y = y * cosf + jnp.where(first_half, up, dn) * sins
        if scale != 1.0:
            y = y * scale
        o_ref[:, h * HEAD_DIM:(h + 1) * HEAD_DIM] = y.astype(o_ref.dtype)


def head_norm(x, col_block, n_blocks, width, g, cos, sin, *, rope, scale):
    t = x.shape[0]
    tm = _tile(t, 512, 8)
    n_heads = width // HEAD_DIM
    kern = functools.partial(_headnorm_kernel, n_heads=n_heads, rope=rope, scale=scale)
    tab = pl.BlockSpec((tm, HEAD_DIM), lambda i, j: (i, 0))
    return pl.pallas_call(
        kern,
        grid=(t // tm, n_blocks),
        in_specs=[
            pl.BlockSpec((tm, width), lambda i, j: (i, col_block + j)),
            pl.BlockSpec((1, HEAD_DIM), lambda i, j: (0, 0)),
            tab, tab,
        ],
        out_specs=pl.BlockSpec((tm, width), lambda i, j: (i, j)),
        out_shape=jax.ShapeDtypeStruct((t, n_blocks * width), BF16),
        compiler_params=_params(("parallel", "parallel"), 32),
        name="head_norm_rope",
    )(x, g, cos, sin)


def rope_tables(n_tokens):
    n_rows = n_tokens // GRID_W
    row = jnp.repeat(jnp.arange(n_rows, dtype=jnp.int32), GRID_W)
    col = jnp.tile(jnp.arange(GRID_W, dtype=jnp.int32), n_rows)
    pos = jnp.stack([row, col], axis=-1).astype(F32)
    inv_freq = ROPE_THETA ** (-jnp.arange(0, ROPE_AXIS_DIM, 2, dtype=F32) / ROPE_AXIS_DIM)
    ang = pos[:, :, None] * inv_freq
    cos, sin = jnp.cos(ang), jnp.sin(ang)
    cosf = jnp.concatenate([cos, cos], axis=-1).reshape(n_tokens, HEAD_DIM)
    sins = jnp.concatenate([-sin, sin], axis=-1).reshape(n_tokens, HEAD_DIM)
    return cosf, sins


def _attn_kernel(q_ref, k_ref, vt_ref, o_ref, acc_sc, *, group, n_chunks):
    tq = q_ref.shape[0]
    for g in range(group):
        qg = q_ref[:, g * HEAD_DIM:(g + 1) * HEAD_DIM]
        acc_sc[...] = jnp.zeros_like(acc_sc)

        def body(c, carry):
            m, l = carry
            kc = k_ref[0, c]
            s = lax.dot_general(kc, qg, (((1,), (1,)), ((), ())),
                                preferred_element_type=F32)
            m_new = jnp.maximum(m, jnp.max(s, axis=0, keepdims=True))
            p = jnp.exp(s - m_new)
            alpha = jnp.exp(m - m_new)
            l_new = alpha * l + jnp.sum(p, axis=0, keepdims=True)
            pv = jnp.dot(vt_ref[0, c], p.astype(BF16), preferred_element_type=F32)
            acc_sc[...] = alpha * acc_sc[...] + pv
            return m_new, l_new

        m0 = jnp.full((1, tq), -jnp.inf, F32)
        l0 = jnp.zeros((1, tq), F32)
        _, l = lax.fori_loop(0, n_chunks, body, (m0, l0))
        out_t = acc_sc[...] / l
        o_ref[:, g * HEAD_DIM:(g + 1) * HEAD_DIM] = out_t.T.astype(o_ref.dtype)


def gqa_attention(q, k_all, v_all, n_kv_heads):
    t, qw = q.shape
    tk_all = k_all.shape[0]
    group = qw // HEAD_DIM // n_kv_heads
    gw = group * HEAD_DIM
    tq = _tile(t, 512)
    tkc = _tile(tk_all, 1280)
    nch = tk_all // tkc
    k4 = k_all.reshape(nch, tkc, n_kv_heads, HEAD_DIM).transpose(2, 0, 1, 3)
    vt4 = v_all.reshape(nch, tkc, n_kv_heads, HEAD_DIM).transpose(2, 0, 3, 1)
    kern = functools.partial(_attn_kernel, group=group, n_chunks=nch)
    return pl.pallas_call(
        kern,
        grid=(n_kv_heads, t // tq),
        in_specs=[
            pl.BlockSpec((tq, gw), lambda h, i: (i, h)),
            pl.BlockSpec((1, nch, tkc, HEAD_DIM), lambda h, i: (h, 0, 0, 0)),
            pl.BlockSpec((1, nch, HEAD_DIM, tkc), lambda h, i: (h, 0, 0, 0)),
        ],
        out_specs=pl.BlockSpec((tq, gw), lambda h, i: (i, h)),
        out_shape=jax.ShapeDtypeStruct((t, qw), BF16),
        scratch_shapes=[pltpu.VMEM((HEAD_DIM, tq), F32)],
        compiler_params=_params(("parallel", "parallel"), 48),
        name="gqa_flash_attention",
    )(q, k4, vt4)


def _conv_kernel(a_ref, g_ref, ap_ref, gp_ref, an_ref, gn_ref, w_ref, b_ref, lg_ref, lb_ref,
                 o_ref, y_sc, c_sc, *, ksize, rows, lanes):
    i = pl.program_id(0)
    n = pl.num_programs(0)
    tt, cw = o_ref.shape

    def glu(a, g):
        a = a.astype(F32)
        g = g.astype(F32)
        return a * jax.nn.sigmoid(g)

    y_sc[HALO:HALO + tt, :] = glu(a_ref[...], g_ref[...])
    y_sc[0:HALO, :] = jnp.where(i > 0, glu(ap_ref[...], gp_ref[...]), 0.0)
    y_sc[HALO + tt:HALO + tt + HALO, :] = jnp.where(i < n - 1, glu(an_ref[...], gn_ref[...]), 0.0)

    base = HALO - ksize // 2
    for c0 in range(0, cw, lanes):
        wc = w_ref[:, c0:c0 + lanes]
        bc = b_ref[:, c0:c0 + lanes]
        for r0 in range(0, tt, rows):
            acc = jnp.broadcast_to(bc, (rows, lanes))
            for j in range(ksize):
                acc = acc + wc[j:j + 1, :] * y_sc[r0 + base + j:r0 + base + j + rows, c0:c0 + lanes]
            c_sc[r0:r0 + rows, c0:c0 + lanes] = acc

    y = c_sc[...]
    mu = jnp.mean(y, axis=-1, keepdims=True)
    yc = y - mu
    var = jnp.mean(yc * yc, axis=-1, keepdims=True)
    z = yc * lax.rsqrt(var + EPS) * lg_ref[...] + lb_ref[...]
    o_ref[...] = (z * jax.nn.sigmoid(z)).astype(o_ref.dtype)


def conformer_conv_pre(u, dw_w, dw_b, ln_g, ln_b):
    t, c2 = u.shape
    cw = c2 // 2
    ksize = dw_w.shape[0]
    assert ksize // 2 <= HALO - 1
    tt = _tile(t, 256, HALO)
    hb = tt // HALO
    n_hb = t // HALO
    cur_a = pl.BlockSpec((tt, cw), lambda i: (i, 0))
    cur_g = pl.BlockSpec((tt, cw), lambda i: (i, 1))
    prev_a = pl.BlockSpec((HALO, cw), lambda i: (jnp.maximum(i * hb - 1, 0), 0))
    prev_g = pl.BlockSpec((HALO, cw), lambda i: (jnp.maximum(i * hb - 1, 0), 1))
    next_a = pl.BlockSpec((HALO, cw), lambda i: (jnp.minimum((i + 1) * hb, n_hb - 1), 0))
    next_g = pl.BlockSpec((HALO, cw), lambda i: (jnp.minimum((i + 1) * hb, n_hb - 1), 1))
    vec = pl.BlockSpec((1, cw), lambda i: (0, 0))
    kern = functools.partial(_conv_kernel, ksize=ksize, rows=min(64, tt), lanes=min(512, cw))
    return pl.pallas_call(
        kern,
        grid=(t // tt,),
        in_specs=[cur_a, cur_g, prev_a, prev_g, next_a, next_g,
                  pl.BlockSpec((ksize, cw), lambda i: (0, 0)), vec, vec, vec],
        out_specs=pl.BlockSpec((tt, cw), lambda i: (i, 0)),
        out_shape=jax.ShapeDtypeStruct((t, cw), BF16),
        scratch_shapes=[pltpu.VMEM((tt + 2 * HALO, cw), F32), pltpu.VMEM((tt, cw), F32)],
        compiler_params=_params(("parallel",), 40),
        name="conformer_conv",
    )(u, u, u, u, u, u, dw_w, dw_b, ln_g, ln_b)


def _merge_kernel(a_ref, wa_ref, b_ref, wb_ref, ga_ref, gb_ref, o_ref):
    pa = jnp.dot(a_ref[...], wa_ref[...], preferred_element_type=F32)
    pb = jnp.dot(b_ref[...], wb_ref[...], preferred_element_type=F32)
    ga = jax.nn.sigmoid(ga_ref[...].astype(F32))
    gb = jax.nn.sigmoid(gb_ref[...].astype(F32))
    o_ref[...] = (ga * pa + gb * pb).astype(o_ref.dtype)


def gated_merge(attn, w_attn_o, conv, w_conv_o, gates):
    t, ka = attn.shape
    kb = conv.shape[1]
    d = w_attn_o.shape[1]
    tm = _tile(t, 512, 8)
    tn = _tile(d, 1024)
    nb = d // tn
    return pl.pallas_call(
        _merge_kernel,
        grid=(nb, t // tm),
        in_specs=[
            pl.BlockSpec((tm, ka), lambda j, i: (i, 0)),
            pl.BlockSpec((ka, tn), lambda j, i: (0, j)),
            pl.BlockSpec((tm, kb), lambda j, i: (i, 0)),
            pl.BlockSpec((kb, tn), lambda j, i: (0, j)),
            pl.BlockSpec((tm, tn), lambda j, i: (i, j)),
            pl.BlockSpec((tm, tn), lambda j, i: (i, nb + j)),
        ],
        out_specs=pl.BlockSpec((tm, tn), lambda j, i: (i, j)),
        out_shape=jax.ShapeDtypeStruct((t, d), BF16),
        compiler_params=_params(("parallel", "parallel"), 48),
        name="gated_merge",
    )(attn, w_attn_o, conv, w_conv_o, gates, gates)


def _mm_resid_kernel(a_ref, w_ref, x_ref, g_ref, o_ref):
    acc = jnp.dot(a_ref[...], w_ref[...], preferred_element_type=F32)
    o_ref[...] = x_ref[...] + g_ref[...] * acc


def matmul_gated_residual(a, w, x, gate):
    m, k = a.shape
    n = w.shape[1]
    tm = _tile(m, 1024, 8)
    tn = _tile(n, 1024)
    return pl.pallas_call(
        _mm_resid_kernel,
        grid=(n // tn, m // tm),
        in_specs=[
            pl.BlockSpec((tm, k), lambda j, i: (i, 0)),
            pl.BlockSpec((k, tn), lambda j, i: (0, j)),
            pl.BlockSpec((tm, tn), lambda j, i: (i, j)),
            pl.BlockSpec((1, tn), lambda j, i: (0, j)),
        ],
        out_specs=pl.BlockSpec((tm, tn), lambda j, i: (i, j)),
        out_shape=jax.ShapeDtypeStruct((m, n), F32),
        compiler_params=_params(("parallel", "parallel"), 56),
        name="out_proj_residual",
    )(a, w, x, gate)


def _router_kernel(x_ref, g_ref, sc_ref, sh_ref, wh_ref, wl_ref, h_ref, aff_ref, *, n_experts):
    x = x_ref[...]
    ms = jnp.mean(x * x, axis=-1, keepdims=True)
    h = x * lax.rsqrt(ms + EPS) * g_ref[...]
    h = h * (1.0 + sc_ref[...]) + sh_ref[...]
    hb = h.astype(BF16)
    h_ref[...] = hb
    hl = (h - hb.astype(F32)).astype(BF16)
    logits = (jnp.dot(hb, wh_ref[...], preferred_element_type=F32)
              + jnp.dot(hb, wl_ref[...], preferred_element_type=F32)
              + jnp.dot(hl, wh_ref[...], preferred_element_type=F32))
    lane = lax.broadcasted_iota(jnp.int32, logits.shape, 1)
    logits = jnp.where(lane < n_experts, logits, -jnp.inf)
    mx = jnp.max(logits, axis=-1, keepdims=True)
    e = jnp.exp(logits - mx)
    aff_ref[...] = e / jnp.sum(e, axis=-1, keepdims=True)


def norm_router(x, g, scale, shift, w_router):
    t, d = x.shape
    n_experts = w_router.shape[1]
    wp = jnp.zeros((d, LANE), F32).at[:, :n_experts].set(w_router)
    wh = wp.astype(BF16)
    wl = (wp - wh.astype(F32)).astype(BF16)
    tm = _tile(t, 256, 8)
    vec = pl.BlockSpec((1, d), lambda i: (0, 0))
    wspec = pl.BlockSpec((d, LANE), lambda i: (0, 0))
    kern = functools.partial(_router_kernel, n_experts=n_experts)
    return pl.pallas_call(
        kern,
        grid=(t // tm,),
        in_specs=[pl.BlockSpec((tm, d), lambda i: (i, 0)), vec, vec, vec, wspec, wspec],
        out_specs=[pl.BlockSpec((tm, d), lambda i: (i, 0)), pl.BlockSpec((tm, LANE), lambda i: (i, 0))],
        out_shape=[jax.ShapeDtypeStruct((t, d), BF16), jax.ShapeDtypeStruct((t, LANE), F32)],
        compiler_params=_params(("parallel",), 40),
        name="norm_router",
    )(x, g, scale, shift, wh, wl)


def _expert_up_kernel(x_ref, wg_ref, wu_ref, o_ref):
    x = x_ref[0]
    g = jnp.dot(x, wg_ref[0].astype(BF16), preferred_element_type=F32)
    u = jnp.dot(x, wu_ref[0].astype(BF16), preferred_element_type=F32)
    o_ref[0] = (g * jax.nn.sigmoid(g) * u).astype(o_ref.dtype)


def expert_up(xg, w_gate, w_up):
    e, c, d = xg.shape
    f = w_gate.shape[2]
    tf = _tile(f, 256)
    wspec = pl.BlockSpec((1, d, tf), lambda i, j: (i, 0, j))
    return pl.pallas_call(
        _expert_up_kernel,
        grid=(e, f // tf),
        in_specs=[pl.BlockSpec((1, c, d), lambda i, j: (i, 0, 0), pipeline_mode=pl.Buffered(1)),
                  wspec, wspec],
        out_specs=pl.BlockSpec((1, c, tf), lambda i, j: (i, 0, j)),
        out_shape=jax.ShapeDtypeStruct((e, c, f), BF16),
        compiler_params=_params(("parallel", "arbitrary"), 56),
        name="expert_up",
    )(xg, w_gate, w_up)


def _expert_down_kernel(h_ref, w_ref, v_ref, o_ref):
    y = jnp.dot(h_ref[0], w_ref[0].astype(BF16), preferred_element_type=F32)
    o_ref[0] = y * v_ref[0]


def expert_down(hid, w_down, vals):
    e, c, f = hid.shape
    d = w_down.shape[2]
    tn = _tile(d, 512)
    return pl.pallas_call(
        _expert_down_kernel,
        grid=(e, d // tn),
        in_specs=[pl.BlockSpec((1, c, f), lambda i, j: (i, 0, 0)),
                  pl.BlockSpec((1, f, tn), lambda i, j: (i, 0, j)),
                  pl.BlockSpec((1, c, 1), lambda i, j: (i, 0, 0))],
        out_specs=pl.BlockSpec((1, c, tn), lambda i, j: (i, 0, j)),
        out_shape=jax.ShapeDtypeStruct((e, c, d), F32),
        compiler_params=_params(("parallel", "arbitrary"), 48),
        name="expert_down",
    )(hid, w_down, vals)


def _final_kernel(x_ref, f_ref, g2_ref, fg_ref, o_ref):
    x = x_ref[...] + g2_ref[...] * f_ref[...]
    ms = jnp.mean(x * x, axis=-1, keepdims=True)
    o_ref[...] = x * lax.rsqrt(ms + EPS) * fg_ref[...]


def final_norm(x, ffn, gate, final_g):
    t, d = x.shape
    tm = _tile(t, 256, 8)
    vec = pl.BlockSpec((1, d), lambda i: (0, 0))
    row = pl.BlockSpec((tm, d), lambda i: (i, 0))
    return pl.pallas_call(
        _final_kernel,
        grid=(t // tm,),
        in_specs=[row, row, vec, vec],
        out_specs=row,
        out_shape=jax.ShapeDtypeStruct((t, d), F32),
        compiler_params=_params(("parallel",), 40),
        name="final_residual_norm",
    )(x, ffn, gate, final_g)


def kernel(x, c, ctx, c_ctx, w_ada, b_ada, norm1_g, w_in, b_in, q_norm_g, k_norm_g, w_attn_o,
           dw_w, dw_b, conv_ln_g, conv_ln_b, w_conv_o, w_out, norm2_g, w_router, w_gate, w_up,
           w_down, final_g):
    batch, t, d = x.shape
    depth = w_ada.shape[0]
    assert batch == 1 and depth == 1
    q_w = w_attn_o.shape[1]
    conv_w = w_conv_o.shape[1]
    in_w = w_in.shape[2]
    kv_w = (in_w - q_w - 2 * conv_w - 2 * d) // 2
    n_kv = kv_w // HEAD_DIM
    n_experts = w_router.shape[2]
    cap = CAPACITY_FACTOR * t // n_experts
    o1, o2, o3, o4 = q_w, q_w + kv_w, q_w + 2 * kv_w, q_w + 2 * kv_w + 2 * conv_w

    row = lambda v: v.reshape(1, -1)
    x2, ctx2 = x[0], ctx[0]

    cvecs_t = jnp.stack([c[0], c_ctx], axis=1)
    mod = ada_modulation(cvecs_t, w_ada[0], row(b_ada[0]))
    sh1, sc1, g1, sh2, sc2, g2 = [mod[0:1, i * d:(i + 1) * d] for i in range(N_MOD)]
    csh1, csc1 = mod[1:2, 0:d], mod[1:2, d:2 * d]

    h_lat = norm_modulate(x2, row(norm1_g[0]), sc1, sh1)
    h_ctx = norm_modulate(ctx2, row(norm1_g[0]), csc1, csh1)

    w_in0, b_in0 = w_in[0], row(b_in[0])
    w_qkv = w_in0[:, :o3].astype(BF16)
    qkv = matmul_bias(h_lat, w_qkv, b_in0[:, :o3])
    u = matmul_bias(h_lat, w_in0[:, o3:o4].astype(BF16), b_in0[:, o3:o4])
    gates = matmul_bias(h_lat, w_in0[:, o4:].astype(BF16), b_in0[:, o4:])
    kv_ctx = matmul_bias(h_ctx, w_qkv[:, o1:o3], b_in0[:, o1:o3])

    cosf, sins = rope_tables(t)
    qg, kg = row(q_norm_g[0]), row(k_norm_g[0])
    gw = q_w // n_kv
    q_r = head_norm(qkv, 0, n_kv, gw, qg, cosf, sins, rope=True, scale=HEAD_DIM ** -0.5)
    k_r = head_norm(qkv, q_w // kv_w, 1, kv_w, kg, cosf, sins, rope=True, scale=1.0)
    tc = ctx2.shape[0]
    k_ctx = head_norm(kv_ctx, 0, 1, kv_w, kg, cosf[:tc], sins[:tc], rope=False, scale=1.0)
    k_all = jnp.concatenate([k_ctx, k_r], axis=0)
    v_all = jnp.concatenate([kv_ctx[:, kv_w:], qkv[:, o2:o3]], axis=0)
    attn = gqa_attention(q_r, k_all, v_all, n_kv)

    conv = conformer_conv_pre(u, dw_w[0], row(dw_b[0]), row(conv_ln_g[0]), row(conv_ln_b[0]))
    merged = gated_merge(attn, w_attn_o[0].astype(BF16), conv, w_conv_o[0].astype(BF16), gates)
    x1 = matmul_gated_residual(merged, w_out[0].astype(BF16), x2, g1)

    h2, aff = norm_router(x1, row(norm2_g[0]), sc2, sh2, w_router[0])
    aff_t = aff[:, :n_experts].T
    vals, idx = lax.top_k(aff_t, cap)
    xg = h2[idx]
    hid = expert_up(xg, w_gate[0], w_up[0])
    y = expert_down(hid, w_down[0], vals[..., None])
    ffn = jnp.zeros((t, d), F32).at[idx.reshape(-1)].add(y.reshape(-1, d))
    out = final_norm(x1, ffn, g2, row(final_g))
    return out[None]
```

```python
import functools

import jax
import jax.numpy as jnp
from jax import lax
from jax.experimental import pallas as pl
from jax.experimental.pallas import tpu as pltpu

F32 = jnp.float32
BF16 = jnp.bfloat16

HEAD_DIM = 128
GRID_W = 64
ROPE_THETA = 10000.0
ROPE_AXIS_DIM = HEAD_DIM // 2
EPS = 1e-6
CAPACITY_FACTOR = 2
N_MOD = 6
HALO = 16
LANE = 128
MIB = 1024 * 1024
LOG2E = 1.4426950408889634


def _tile(n, pref, mult=LANE):
    if n <= pref:
        return n
    t = (pref // mult) * mult
    while t >= mult:
        if n % t == 0:
            return t
        t -= mult
    return n


def _params(sem, vmem_mib):
    return pltpu.CompilerParams(dimension_semantics=sem, vmem_limit_bytes=vmem_mib * MIB)


def _ada_kernel(s_ref, w_ref, b_ref, o_ref):
    k = pl.program_id(1)

    @pl.when(k == 0)
    def _():
        o_ref[...] = jnp.broadcast_to(b_ref[...], o_ref.shape)

    s = s_ref[...]
    s = s * jax.nn.sigmoid(s)
    w = w_ref[...]
    r0 = jnp.sum(s[:, 0:1] * w, axis=0, keepdims=True)
    r1 = jnp.sum(s[:, 1:2] * w, axis=0, keepdims=True)
    o_ref[0:1, :] += r0
    o_ref[1:2, :] += r1


def ada_modulation(cvecs_t, w_ada, b_ada):
    d, n = w_ada.shape
    tk = _tile(d, 256, 8)
    tn = _tile(n, 2048)
    return pl.pallas_call(
        _ada_kernel,
        grid=(n // tn, d // tk),
        in_specs=[
            pl.BlockSpec((tk, 2), lambda j, k: (k, 0)),
            pl.BlockSpec((tk, tn), lambda j, k: (k, j)),
            pl.BlockSpec((1, tn), lambda j, k: (0, j)),
        ],
        out_specs=pl.BlockSpec((2, tn), lambda j, k: (0, j)),
        out_shape=jax.ShapeDtypeStruct((2, n), F32),
        compiler_params=_params(("parallel", "arbitrary"), 32),
        name="ada_modulation",
    )(cvecs_t, w_ada, b_ada)


def _normmod_kernel(x_ref, g_ref, sc_ref, sh_ref, o_ref):
    x = x_ref[...].astype(F32)
    ms = jnp.mean(x * x, axis=-1, keepdims=True)
    y = x * lax.rsqrt(ms + EPS) * g_ref[...]
    o_ref[...] = (y * (1.0 + sc_ref[...]) + sh_ref[...]).astype(o_ref.dtype)


def norm_modulate(x, g, scale, shift, out_dtype=BF16):
    t, d = x.shape
    tm = _tile(t, 256, 8)
    vec = pl.BlockSpec((1, d), lambda i: (0, 0))
    return pl.pallas_call(
        _normmod_kernel,
        grid=(t // tm,),
        in_specs=[pl.BlockSpec((tm, d), lambda i: (i, 0)), vec, vec, vec],
        out_specs=pl.BlockSpec((tm, d), lambda i: (i, 0)),
        out_shape=jax.ShapeDtypeStruct((t, d), out_dtype),
        compiler_params=_params(("parallel",), 40),
        name="norm_modulate",
    )(x, g, scale, shift)


def _mm_bias_kernel(a_ref, w_ref, b_ref, o_ref):
    acc = jnp.dot(a_ref[...], w_ref[...], preferred_element_type=F32)
    o_ref[...] = (acc + b_ref[...]).astype(o_ref.dtype)


def matmul_bias(a, w, b, out_dtype=BF16, tm_pref=1024, tn_pref=1024):
    m, k = a.shape
    n = w.shape[1]
    tm = _tile(m, tm_pref, 8)
    tn = _tile(n, tn_pref)
    return pl.pallas_call(
        _mm_bias_kernel,
        grid=(n // tn, m // tm),
        in_specs=[
            pl.BlockSpec((tm, k), lambda j, i: (i, 0)),
            pl.BlockSpec((k, tn), lambda j, i: (0, j)),
            pl.BlockSpec((1, tn), lambda j, i: (0, j)),
        ],
        out_specs=pl.BlockSpec((tm, tn), lambda j, i: (i, j)),
        out_shape=jax.ShapeDtypeStruct((m, n), out_dtype),
        compiler_params=_params(("parallel", "parallel"), 56),
        name="matmul_bias",
    )(a, w, b)


def _headnorm_kernel(x_ref, g_ref, cos_ref, sin_ref, o_ref, *, n_heads, rope, scale):
    g = g_ref[...]
    if rope:
        cosf = cos_ref[...]
        sins = sin_ref[...]
        lane = lax.broadcasted_iota(jnp.int32, cosf.shape, 1)
        first_half = (lane % (ROPE_AXIS_DIM)) < (ROPE_AXIS_DIM // 2)
    for h in range(n_heads):
        x = x_ref[:, h * HEAD_DIM:(h + 1) * HEAD_DIM].astype(F32)
        ms = jnp.mean(x * x, axis=-1, keepdims=True)
        y = x * lax.rsqrt(ms + EPS) * g
        if rope:
            up = pltpu.roll(y, HEAD_DIM - ROPE_AXIS_DIM // 2, axis=1)
            dn = pltpu.roll(y, ROPE_AXIS_DIM // 2, axis=1)
            y = y * cosf + jnp.where(first_half, up, dn) * sins
        if scale != 1.0:
            y = y * scale
        o_ref[:, h * HEAD_DIM:(h + 1) * HEAD_DIM] = y.astype(o_ref.dtype)


def head_norm(x, col_block, n_blocks, width, g, cos, sin, *, rope, scale):
    t = x.shape[0]
    tm = _tile(t, 512, 8)
    n_heads = width // HEAD_DIM
    kern = functools.partial(_headnorm_kernel, n_heads=n_heads, rope=rope, scale=scale)
    tab = pl.BlockSpec((tm, HEAD_DIM), lambda i, j: (i, 0))
    return pl.pallas_call(
        kern,
        grid=(t // tm, n_blocks),
        in_specs=[
            pl.BlockSpec((tm, width), lambda i, j: (i, col_block + j)),
            pl.BlockSpec((1, HEAD_DIM), lambda i, j: (0, 0)),
            tab, tab,
        ],
        out_specs=pl.BlockSpec((tm, width), lambda i, j: (i, j)),
        out_shape=jax.ShapeDtypeStruct((t, n_blocks * width), BF16),
        compiler_params=_params(("parallel", "parallel"), 32),
        name="head_norm_rope",
    )(x, g, cos, sin)


def rope_tables(n_tokens):
    n_rows = n_tokens // GRID_W
    row = jnp.repeat(jnp.arange(n_rows, dtype=jnp.int32), GRID_W)
    col = jnp.tile(jnp.arange(GRID_W, dtype=jnp.int32), n_rows)
    pos = jnp.stack([row, col], axis=-1).astype(F32)
    inv_freq = ROPE_THETA ** (-jnp.arange(0, ROPE_AXIS_DIM, 2, dtype=F32) / ROPE_AXIS_DIM)
    ang = pos[:, :, None] * inv_freq
    cos, sin = jnp.cos(ang), jnp.sin(ang)
    cosf = jnp.concatenate([cos, cos], axis=-1).reshape(n_tokens, HEAD_DIM)
    sins = jnp.concatenate([-sin, sin], axis=-1).reshape(n_tokens, HEAD_DIM)
    return cosf, sins


def _attn_kernel(b_ref, q_ref, k_ref, vt_ref, o_ref, acc_sc, *, group, n_chunks, bounded):
    tq = q_ref.shape[0]
    acc_sc[...] = jnp.zeros_like(acc_sc)
    qs = [q_ref[:, g * HEAD_DIM:(g + 1) * HEAD_DIM] for g in range(group)]
    shift = b_ref[0, 0]

    def body(c, carry):
        ms, ls = carry
        kc = k_ref[0, c]
        vc = vt_ref[0, c]
        new_m, new_l = [], []
        for g in range(group):
            s = lax.dot_general(kc, qs[g], (((1,), (1,)), ((), ())),
                                preferred_element_type=F32)
            if bounded:
                p = jnp.exp2(s - shift)
                new_m.append(ms[g])
                new_l.append(ls[g] + jnp.sum(p, axis=0, keepdims=True))
                acc_sc[g] += jnp.dot(vc, p.astype(BF16), preferred_element_type=F32)
            else:
                m_new = jnp.maximum(ms[g], jnp.max(s, axis=0, keepdims=True))
                p = jnp.exp2(s - m_new)
                alpha = jnp.exp2(ms[g] - m_new)
                new_m.append(m_new)
                new_l.append(alpha * ls[g] + jnp.sum(p, axis=0, keepdims=True))
                pv = jnp.dot(vc, p.astype(BF16), preferred_element_type=F32)
                acc_sc[g] = alpha * acc_sc[g] + pv
        return tuple(new_m), tuple(new_l)

    m0 = tuple(jnp.full((1, tq), -jnp.inf, F32) for _ in range(group))
    l0 = tuple(jnp.zeros((1, tq), F32) for _ in range(group))
    _, ls = lax.fori_loop(0, n_chunks, body, (m0, l0))
    for g in range(group):
        out_t = acc_sc[g] / ls[g]
        o_ref[:, g * HEAD_DIM:(g + 1) * HEAD_DIM] = out_t.T.astype(o_ref.dtype)


MAX_SAFE_SCORE_BOUND = 60.0


def gqa_attention(q, k_all, v_all, n_kv_heads, score_bound):
    t, qw = q.shape
    tk_all = k_all.shape[0]
    group = qw // HEAD_DIM // n_kv_heads
    gw = group * HEAD_DIM
    tq = _tile(t, 512)
    tkc = _tile(tk_all, 1280)
    nch = tk_all // tkc
    k4 = k_all.reshape(nch, tkc, n_kv_heads, HEAD_DIM).transpose(2, 0, 1, 3)
    vt4 = v_all.reshape(nch, tkc, n_kv_heads, HEAD_DIM).transpose(2, 0, 3, 1)
    shift = score_bound.reshape(1, 1).astype(F32)

    def call(bounded):
        kern = functools.partial(_attn_kernel, group=group, n_chunks=nch, bounded=bounded)
        return pl.pallas_call(
            kern,
            grid=(n_kv_heads, t // tq),
            in_specs=[
                pl.BlockSpec(memory_space=pltpu.SMEM),
                pl.BlockSpec((tq, gw), lambda h, i: (i, h)),
                pl.BlockSpec((1, nch, tkc, HEAD_DIM), lambda h, i: (h, 0, 0, 0)),
                pl.BlockSpec((1, nch, HEAD_DIM, tkc), lambda h, i: (h, 0, 0, 0)),
            ],
            out_specs=pl.BlockSpec((tq, gw), lambda h, i: (i, h)),
            out_shape=jax.ShapeDtypeStruct((t, qw), BF16),
            scratch_shapes=[pltpu.VMEM((group, HEAD_DIM, tq), F32)],
            compiler_params=_params(("parallel", "parallel"), 48),
            name="gqa_flash_attention_bounded" if bounded else "gqa_flash_attention_online",
        )(shift, q, k4, vt4)

    return lax.cond(score_bound < MAX_SAFE_SCORE_BOUND, lambda: call(True), lambda: call(False))


def _conv_kernel(a_ref, g_ref, ap_ref, gp_ref, an_ref, gn_ref, w_ref, b_ref, lg_ref, lb_ref,
                 o_ref, y_sc, c_sc, *, ksize, rows, lanes):
    i = pl.program_id(0)
    n = pl.num_programs(0)
    tt, cw = o_ref.shape

    def glu(a, g):
        a = a.astype(F32)
        g = g.astype(F32)
        return a * jax.nn.sigmoid(g)

    y_sc[HALO:HALO + tt, :] = glu(a_ref[...], g_ref[...])
    y_sc[0:HALO, :] = jnp.where(i > 0, glu(ap_ref[...], gp_ref[...]), 0.0)
    y_sc[HALO + tt:HALO + tt + HALO, :] = jnp.where(i < n - 1, glu(an_ref[...], gn_ref[...]), 0.0)

    base = HALO - ksize // 2
    for c0 in range(0, cw, lanes):
        wc = w_ref[:, c0:c0 + lanes]
        bc = b_ref[:, c0:c0 + lanes]
        for r0 in range(0, tt, rows):
            acc = jnp.broadcast_to(bc, (rows, lanes))
            for j in range(ksize):
                acc = acc + wc[j:j + 1, :] * y_sc[r0 + base + j:r0 + base + j + rows, c0:c0 + lanes]
            c_sc[r0:r0 + rows, c0:c0 + lanes] = acc

    y = c_sc[...]
    mu = jnp.mean(y, axis=-1, keepdims=True)
    yc = y - mu
    var = jnp.mean(yc * yc, axis=-1, keepdims=True)
    z = yc * lax.rsqrt(var + EPS) * lg_ref[...] + lb_ref[...]
    o_ref[...] = (z * jax.nn.sigmoid(z)).astype(o_ref.dtype)


def conformer_conv_pre(u, dw_w, dw_b, ln_g, ln_b):
    t, c2 = u.shape
    cw = c2 // 2
    ksize = dw_w.shape[0]
    assert ksize // 2 <= HALO - 1
    tt = _tile(t, 256, HALO)
    hb = tt // HALO
    n_hb = t // HALO
    cur_a = pl.BlockSpec((tt, cw), lambda i: (i, 0))
    cur_g = pl.BlockSpec((tt, cw), lambda i: (i, 1))
    prev_a = pl.BlockSpec((HALO, cw), lambda i: (jnp.maximum(i * hb - 1, 0), 0))
    prev_g = pl.BlockSpec((HALO, cw), lambda i: (jnp.maximum(i * hb - 1, 0), 1))
    next_a = pl.BlockSpec((HALO, cw), lambda i: (jnp.minimum((i + 1) * hb, n_hb - 1), 0))
    next_g = pl.BlockSpec((HALO, cw), lambda i: (jnp.minimum((i + 1) * hb, n_hb - 1), 1))
    vec = pl.BlockSpec((1, cw), lambda i: (0, 0))
    kern = functools.partial(_conv_kernel, ksize=ksize, rows=min(64, tt), lanes=min(512, cw))
    return pl.pallas_call(
        kern,
        grid=(t // tt,),
        in_specs=[cur_a, cur_g, prev_a, prev_g, next_a, next_g,
                  pl.BlockSpec((ksize, cw), lambda i: (0, 0)), vec, vec, vec],
        out_specs=pl.BlockSpec((tt, cw), lambda i: (i, 0)),
        out_shape=jax.ShapeDtypeStruct((t, cw), BF16),
        scratch_shapes=[pltpu.VMEM((tt + 2 * HALO, cw), F32), pltpu.VMEM((tt, cw), F32)],
        compiler_params=_params(("parallel",), 40),
        name="conformer_conv",
    )(u, u, u, u, u, u, dw_w, dw_b, ln_g, ln_b)


def _merge_kernel(a_ref, wa_ref, b_ref, wb_ref, ga_ref, gb_ref, o_ref):
    pa = jnp.dot(a_ref[...], wa_ref[...], preferred_element_type=F32)
    pb = jnp.dot(b_ref[...], wb_ref[...], preferred_element_type=F32)
    ga = jax.nn.sigmoid(ga_ref[...].astype(F32))
    gb = jax.nn.sigmoid(gb_ref[...].astype(F32))
    o_ref[...] = (ga * pa + gb * pb).astype(o_ref.dtype)


def gated_merge(attn, w_attn_o, conv, w_conv_o, gates):
    t, ka = attn.shape
    kb = conv.shape[1]
    d = w_attn_o.shape[1]
    tm = _tile(t, 512, 8)
    tn = _tile(d, 1024)
    nb = d // tn
    return pl.pallas_call(
        _merge_kernel,
        grid=(nb, t // tm),
        in_specs=[
            pl.BlockSpec((tm, ka), lambda j, i: (i, 0)),
            pl.BlockSpec((ka, tn), lambda j, i: (0, j)),
            pl.BlockSpec((tm, kb), lambda j, i: (i, 0)),
            pl.BlockSpec((kb, tn), lambda j, i: (0, j)),
            pl.BlockSpec((tm, tn), lambda j, i: (i, j)),
            pl.BlockSpec((tm, tn), lambda j, i: (i, nb + j)),
        ],
        out_specs=pl.BlockSpec((tm, tn), lambda j, i: (i, j)),
        out_shape=jax.ShapeDtypeStruct((t, d), BF16),
        compiler_params=_params(("parallel", "parallel"), 48),
        name="gated_merge",
    )(attn, w_attn_o, conv, w_conv_o, gates, gates)


def _mm_resid_kernel(a_ref, w_ref, x_ref, g_ref, o_ref):
    acc = jnp.dot(a_ref[...], w_ref[...], preferred_element_type=F32)
    o_ref[...] = x_ref[...] + g_ref[...] * acc


def matmul_gated_residual(a, w, x, gate):
    m, k = a.shape
    n = w.shape[1]
    tm = _tile(m, 1024, 8)
    tn = _tile(n, 1024)
    return pl.pallas_call(
        _mm_resid_kernel,
        grid=(n // tn, m // tm),
        in_specs=[
            pl.BlockSpec((tm, k), lambda j, i: (i, 0)),
            pl.BlockSpec((k, tn), lambda j, i: (0, j)),
            pl.BlockSpec((tm, tn), lambda j, i: (i, j)),
            pl.BlockSpec((1, tn), lambda j, i: (0, j)),
        ],
        out_specs=pl.BlockSpec((tm, tn), lambda j, i: (i, j)),
        out_shape=jax.ShapeDtypeStruct((m, n), F32),
        compiler_params=_params(("parallel", "parallel"), 56),
        name="out_proj_residual",
    )(a, w, x, gate)


def _router_kernel(x_ref, g_ref, sc_ref, sh_ref, wh_ref, wl_ref, h_ref, aff_ref, *, n_experts):
    x = x_ref[...]
    ms = jnp.mean(x * x, axis=-1, keepdims=True)
    h = x * lax.rsqrt(ms + EPS) * g_ref[...]
    h = h * (1.0 + sc_ref[...]) + sh_ref[...]
    hb = h.astype(BF16)
    h_ref[...] = hb
    hl = (h - hb.astype(F32)).astype(BF16)
    logits = (jnp.dot(hb, wh_ref[...], preferred_element_type=F32)
              + jnp.dot(hb, wl_ref[...], preferred_element_type=F32)
              + jnp.dot(hl, wh_ref[...], preferred_element_type=F32))
    lane = lax.broadcasted_iota(jnp.int32, logits.shape, 1)
    logits = jnp.where(lane < n_experts, logits, -jnp.inf)
    mx = jnp.max(logits, axis=-1, keepdims=True)
    e = jnp.exp(logits - mx)
    aff_ref[...] = e / jnp.sum(e, axis=-1, keepdims=True)


def norm_router(x, g, scale, shift, w_router):
    t, d = x.shape
    n_experts = w_router.shape[1]
    wp = jnp.zeros((d, LANE), F32).at[:, :n_experts].set(w_router)
    wh = wp.astype(BF16)
    wl = (wp - wh.astype(F32)).astype(BF16)
    tm = _tile(t, 256, 8)
    vec = pl.BlockSpec((1, d), lambda i: (0, 0))
    wspec = pl.BlockSpec((d, LANE), lambda i: (0, 0))
    kern = functools.partial(_router_kernel, n_experts=n_experts)
    return pl.pallas_call(
        kern,
        grid=(t // tm,),
        in_specs=[pl.BlockSpec((tm, d), lambda i: (i, 0)), vec, vec, vec, wspec, wspec],
        out_specs=[pl.BlockSpec((tm, d), lambda i: (i, 0)), pl.BlockSpec((tm, LANE), lambda i: (i, 0))],
        out_shape=[jax.ShapeDtypeStruct((t, d), BF16), jax.ShapeDtypeStruct((t, LANE), F32)],
        compiler_params=_params(("parallel",), 40),
        name="norm_router",
    )(x, g, scale, shift, wh, wl)


def _expert_up_kernel(x_ref, wg_ref, wu_ref, o_ref):
    x = x_ref[0]
    g = jnp.dot(x, wg_ref[0].astype(BF16), preferred_element_type=F32)
    u = jnp.dot(x, wu_ref[0].astype(BF16), preferred_element_type=F32)
    o_ref[0] = (g * jax.nn.sigmoid(g) * u).astype(o_ref.dtype)


def expert_up(xg, w_gate, w_up):
    e, c, d = xg.shape
    f = w_gate.shape[2]
    tf = _tile(f, 256)
    wspec = pl.BlockSpec((1, d, tf), lambda i, j: (i, 0, j))
    return pl.pallas_call(
        _expert_up_kernel,
        grid=(e, f // tf),
        in_specs=[pl.BlockSpec((1, c, d), lambda i, j: (i, 0, 0), pipeline_mode=pl.Buffered(1)),
                  wspec, wspec],
        out_specs=pl.BlockSpec((1, c, tf), lambda i, j: (i, 0, j)),
        out_shape=jax.ShapeDtypeStruct((e, c, f), BF16),
        compiler_params=_params(("parallel", "arbitrary"), 56),
        name="expert_up",
    )(xg, w_gate, w_up)


def _expert_down_kernel(h_ref, w_ref, v_ref, o_ref):
    y = jnp.dot(h_ref[0], w_ref[0].astype(BF16), preferred_element_type=F32)
    o_ref[0] = y * v_ref[0]


def expert_down(hid, w_down, vals):
    e, c, f = hid.shape
    d = w_down.shape[2]
    tn = _tile(d, 512)
    return pl.pallas_call(
        _expert_down_kernel,
        grid=(e, d // tn),
        in_specs=[pl.BlockSpec((1, c, f), lambda i, j: (i, 0, 0)),
                  pl.BlockSpec((1, f, tn), lambda i, j: (i, 0, j)),
                  pl.BlockSpec((1, c, 1), lambda i, j: (i, 0, 0))],
        out_specs=pl.BlockSpec((1, c, tn), lambda i, j: (i, 0, j)),
        out_shape=jax.ShapeDtypeStruct((e, c, d), F32),
        compiler_params=_params(("parallel", "arbitrary"), 48),
        name="expert_down",
    )(hid, w_down, vals)


def _final_kernel(x_ref, f_ref, g2_ref, fg_ref, o_ref):
    x = x_ref[...] + g2_ref[...] * f_ref[...]
    ms = jnp.mean(x * x, axis=-1, keepdims=True)
    o_ref[...] = x * lax.rsqrt(ms + EPS) * fg_ref[...]


def final_norm(x, ffn, gate, final_g):
    t, d = x.shape
    tm = _tile(t, 256, 8)
    vec = pl.BlockSpec((1, d), lambda i: (0, 0))
    row = pl.BlockSpec((tm, d), lambda i: (i, 0))
    return pl.pallas_call(
        _final_kernel,
        grid=(t // tm,),
        in_specs=[row, row, vec, vec],
        out_specs=row,
        out_shape=jax.ShapeDtypeStruct((t, d), F32),
        compiler_params=_params(("parallel",), 40),
        name="final_residual_norm",
    )(x, ffn, gate, final_g)


def kernel(x, c, ctx, c_ctx, w_ada, b_ada, norm1_g, w_in, b_in, q_norm_g, k_norm_g, w_attn_o,
           dw_w, dw_b, conv_ln_g, conv_ln_b, w_conv_o, w_out, norm2_g, w_router, w_gate, w_up,
           w_down, final_g):
    batch, t, d = x.shape
    depth = w_ada.shape[0]
    assert batch == 1 and depth == 1
    q_w = w_attn_o.shape[1]
    conv_w = w_conv_o.shape[1]
    in_w = w_in.shape[2]
    kv_w = (in_w - q_w - 2 * conv_w - 2 * d) // 2
    n_kv = kv_w // HEAD_DIM
    n_experts = w_router.shape[2]
    cap = CAPACITY_FACTOR * t // n_experts
    o1, o2, o3, o4 = q_w, q_w + kv_w, q_w + 2 * kv_w, q_w + 2 * kv_w + 2 * conv_w

    row = lambda v: v.reshape(1, -1)
    x2, ctx2 = x[0], ctx[0]

    cvecs_t = jnp.stack([c[0], c_ctx], axis=1)
    mod = ada_modulation(cvecs_t, w_ada[0], row(b_ada[0]))
    sh1, sc1, g1, sh2, sc2, g2 = [mod[0:1, i * d:(i + 1) * d] for i in range(N_MOD)]
    csh1, csc1 = mod[1:2, 0:d], mod[1:2, d:2 * d]

    h_lat = norm_modulate(x2, row(norm1_g[0]), sc1, sh1)
    h_ctx = norm_modulate(ctx2, row(norm1_g[0]), csc1, csh1)

    w_in0, b_in0 = w_in[0], row(b_in[0])
    w_qkv = w_in0[:, :o3].astype(BF16)
    qkv = matmul_bias(h_lat, w_qkv, b_in0[:, :o3])
    u = matmul_bias(h_lat, w_in0[:, o3:o4].astype(BF16), b_in0[:, o3:o4])
    gates = matmul_bias(h_lat, w_in0[:, o4:].astype(BF16), b_in0[:, o4:])
    kv_ctx = matmul_bias(h_ctx, w_qkv[:, o1:o3], b_in0[:, o1:o3])

    cosf, sins = rope_tables(t)
    qg, kg = row(q_norm_g[0]), row(k_norm_g[0])
    gw = q_w // n_kv
    q_scale = HEAD_DIM ** -0.5 * LOG2E
    q_r = head_norm(qkv, 0, n_kv, gw, qg, cosf, sins, rope=True, scale=q_scale)
    k_r = head_norm(qkv, q_w // kv_w, 1, kv_w, kg, cosf, sins, rope=True, scale=1.0)
    tc = ctx2.shape[0]
    k_ctx = head_norm(kv_ctx, 0, 1, kv_w, kg, cosf[:tc], sins[:tc], rope=False, scale=1.0)
    k_all = jnp.concatenate([k_ctx, k_r], axis=0)
    v_all = jnp.concatenate([kv_ctx[:, kv_w:], qkv[:, o2:o3]], axis=0)
    score_bound = (1.02 * HEAD_DIM * q_scale) * jnp.max(jnp.abs(q_norm_g[0])) * jnp.max(jnp.abs(k_norm_g[0]))
    attn = gqa_attention(q_r, k_all, v_all, n_kv, score_bound)

    conv = conformer_conv_pre(u, dw_w[0], row(dw_b[0]), row(conv_ln_g[0]), row(conv_ln_b[0]))
    merged = gated_merge(attn, w_attn_o[0].astype(BF16), conv, w_conv_o[0].astype(BF16), gates)
    x1 = matmul_gated_residual(merged, w_out[0].astype(BF16), x2, g1)

    h2, aff = norm_router(x1, row(norm2_g[0]), sc2, sh2, w_router[0])
    aff_t = aff[:, :n_experts].T
    vals, idx = lax.top_k(aff_t, cap)
    xg = h2[idx]
    hid = expert_up(xg, w_gate[0], w_up[0])
    y = expert_down(hid, w_down[0], vals[..., None])
    ffn = jnp.zeros((t, d), F32).at[idx.reshape(-1)].add(y.reshape(-1, d))
    out = final_norm(x1, ffn, g2, row(final_g))
    return out[None]
```

```python
import functools

import jax
import jax.numpy as jnp
from jax import lax
from jax.experimental import pallas as pl
from jax.experimental.pallas import tpu as pltpu

F32 = jnp.float32
BF16 = jnp.bfloat16

HEAD_DIM = 128
GRID_W = 64
ROPE_THETA = 10000.0
ROPE_AXIS_DIM = HEAD_DIM // 2
EPS = 1e-6
CAPACITY_FACTOR = 2
N_MOD = 6
HALO = 16
LANE = 128
SUBLANES = 8
MIB = 1024 * 1024
LOG2E = 1.4426950408889634


def _tile(n, pref, mult=LANE):
    if n <= pref:
        return n
    t = (pref // mult) * mult
    while t >= mult:
        if n % t == 0:
            return t
        t -= mult
    return n


def _params(sem, vmem_mib):
    return pltpu.CompilerParams(dimension_semantics=sem, vmem_limit_bytes=vmem_mib * MIB)


def _ada_kernel(s_ref, w_ref, b_ref, o_ref):
    k = pl.program_id(1)

    @pl.when(k == 0)
    def _():
        o_ref[...] = jnp.broadcast_to(b_ref[...], o_ref.shape)

    s = s_ref[...]
    s = s * jax.nn.sigmoid(s)
    w = w_ref[...]
    r0 = jnp.sum(s[:, 0:1] * w, axis=0, keepdims=True)
    r1 = jnp.sum(s[:, 1:2] * w, axis=0, keepdims=True)
    o_ref[0:1, :] += r0
    o_ref[1:2, :] += r1


def ada_modulation(cvecs_t, w_ada, b_ada):
    d, n = w_ada.shape
    tk = _tile(d, 256, 8)
    tn = _tile(n, 2048)
    return pl.pallas_call(
        _ada_kernel,
        grid=(n // tn, d // tk),
        in_specs=[
            pl.BlockSpec((tk, 2), lambda j, k: (k, 0)),
            pl.BlockSpec((tk, tn), lambda j, k: (k, j)),
            pl.BlockSpec((1, tn), lambda j, k: (0, j)),
        ],
        out_specs=pl.BlockSpec((2, tn), lambda j, k: (0, j)),
        out_shape=jax.ShapeDtypeStruct((2, n), F32),
        compiler_params=_params(("parallel", "arbitrary"), 32),
        name="ada_modulation",
    )(cvecs_t, w_ada, b_ada)


def _normmod_kernel(x_ref, g_ref, sc_ref, sh_ref, o_ref):
    x = x_ref[...].astype(F32)
    ms = jnp.mean(x * x, axis=-1, keepdims=True)
    y = x * lax.rsqrt(ms + EPS) * g_ref[...]
    o_ref[...] = (y * (1.0 + sc_ref[...]) + sh_ref[...]).astype(o_ref.dtype)


def norm_modulate(x, g, scale, shift, out_dtype=BF16):
    t, d = x.shape
    tm = _tile(t, 256, 8)
    vec = pl.BlockSpec((1, d), lambda i: (0, 0))
    return pl.pallas_call(
        _normmod_kernel,
        grid=(t // tm,),
        in_specs=[pl.BlockSpec((tm, d), lambda i: (i, 0)), vec, vec, vec],
        out_specs=pl.BlockSpec((tm, d), lambda i: (i, 0)),
        out_shape=jax.ShapeDtypeStruct((t, d), out_dtype),
        compiler_params=_params(("parallel",), 40),
        name="norm_modulate",
    )(x, g, scale, shift)


def _mm_bias_kernel(a_ref, w_ref, b_ref, o_ref):
    acc = jnp.dot(a_ref[...], w_ref[...], preferred_element_type=F32)
    o_ref[...] = (acc + b_ref[...]).astype(o_ref.dtype)


def matmul_bias(a, w, b, out_dtype=BF16, tm_pref=1024, tn_pref=1024):
    m, k = a.shape
    n = w.shape[1]
    tm = _tile(m, tm_pref, 8)
    tn = _tile(n, tn_pref)
    return pl.pallas_call(
        _mm_bias_kernel,
        grid=(n // tn, m // tm),
        in_specs=[
            pl.BlockSpec((tm, k), lambda j, i: (i, 0)),
            pl.BlockSpec((k, tn), lambda j, i: (0, j)),
            pl.BlockSpec((1, tn), lambda j, i: (0, j)),
        ],
        out_specs=pl.BlockSpec((tm, tn), lambda j, i: (i, j)),
        out_shape=jax.ShapeDtypeStruct((m, n), out_dtype),
        compiler_params=_params(("parallel", "parallel"), 56),
        name="matmul_bias",
    )(a, w, b)


def _headnorm_kernel(x_ref, g_ref, cos_ref, sin_ref, o_ref, *, n_heads, rope, scale):
    g = g_ref[...]
    if rope:
        cosf = cos_ref[...]
        sins = sin_ref[...]
        lane = lax.broadcasted_iota(jnp.int32, cosf.shape, 1)
        first_half = (lane % (ROPE_AXIS_DIM)) < (ROPE_AXIS_DIM // 2)
    for h in range(n_heads):
        x = x_ref[:, h * HEAD_DIM:(h + 1) * HEAD_DIM].astype(F32)
        ms = jnp.mean(x * x, axis=-1, keepdims=True)
        y = x * lax.rsqrt(ms + EPS) * g
        if rope:
            up = pltpu.roll(y, HEAD_DIM - ROPE_AXIS_DIM // 2, axis=1)
            dn = pltpu.roll(y, ROPE_AXIS_DIM // 2, axis=1)
            y = y * cosf + jnp.where(first_half, up, dn) * sins
        if scale != 1.0:
            y = y * scale
        o_ref[:, h * HEAD_DIM:(h + 1) * HEAD_DIM] = y.astype(o_ref.dtype)


def head_norm(x, col_block, n_blocks, width, g, cos, sin, *, rope, scale):
    t = x.shape[0]
    tm = _tile(t, 512, 8)
    n_heads = width // HEAD_DIM
    kern = functools.partial(_headnorm_kernel, n_heads=n_heads, rope=rope, scale=scale)
    tab = pl.BlockSpec((tm, HEAD_DIM), lambda i, j: (i, 0))
    return pl.pallas_call(
        kern,
        grid=(t // tm, n_blocks),
        in_specs=[
            pl.BlockSpec((tm, width), lambda i, j: (i, col_block + j)),
            pl.BlockSpec((1, HEAD_DIM), lambda i, j: (0, 0)),
            tab, tab,
        ],
        out_specs=pl.BlockSpec((tm, width), lambda i, j: (i, j)),
        out_shape=jax.ShapeDtypeStruct((t, n_blocks * width), BF16),
        compiler_params=_params(("parallel", "parallel"), 32),
        name="head_norm_rope",
    )(x, g, cos, sin)


def rope_tables(n_tokens):
    n_rows = n_tokens // GRID_W
    row = jnp.repeat(jnp.arange(n_rows, dtype=jnp.int32), GRID_W)
    col = jnp.tile(jnp.arange(GRID_W, dtype=jnp.int32), n_rows)
    pos = jnp.stack([row, col], axis=-1).astype(F32)
    inv_freq = ROPE_THETA ** (-jnp.arange(0, ROPE_AXIS_DIM, 2, dtype=F32) / ROPE_AXIS_DIM)
    ang = pos[:, :, None] * inv_freq
    cos, sin = jnp.cos(ang), jnp.sin(ang)
    cosf = jnp.concatenate([cos, cos], axis=-1).reshape(n_tokens, HEAD_DIM)
    sins = jnp.concatenate([-sin, sin], axis=-1).reshape(n_tokens, HEAD_DIM)
    return cosf, sins


def _attn_kernel(b_ref, q_ref, k_ref, vt_ref, o_ref, acc_sc, *, group, n_chunks, bounded):
    tq = q_ref.shape[0]
    acc_sc[...] = jnp.zeros_like(acc_sc)
    qs = [q_ref[:, g * HEAD_DIM:(g + 1) * HEAD_DIM] for g in range(group)]
    shift = b_ref[0, 0]

    def body(c, carry):
        ms, ls = carry
        kc = k_ref[0, c]
        vc = vt_ref[0, c]
        new_m, new_l = [], []
        for g in range(group):
            s = lax.dot_general(kc, qs[g], (((1,), (1,)), ((), ())),
                                preferred_element_type=F32)
            if bounded:
                p = jnp.exp2(s - shift)
                new_m.append(ms[g])
                new_l.append(ls[g] + jnp.sum(p, axis=0, keepdims=True))
                acc_sc[g] += jnp.dot(vc, p.astype(BF16), preferred_element_type=F32)
            else:
                m_new = jnp.maximum(ms[g], jnp.max(s, axis=0, keepdims=True))
                p = jnp.exp2(s - m_new)
                alpha = jnp.exp2(ms[g] - m_new)
                new_m.append(m_new)
                new_l.append(alpha * ls[g] + jnp.sum(p, axis=0, keepdims=True))
                pv = jnp.dot(vc, p.astype(BF16), preferred_element_type=F32)
                acc_sc[g] = alpha * acc_sc[g] + pv
        return tuple(new_m), tuple(new_l)

    m0 = tuple(jnp.full((1, tq), -jnp.inf, F32) for _ in range(group))
    l0 = tuple(jnp.zeros((1, tq), F32) for _ in range(group))
    _, ls = lax.fori_loop(0, n_chunks, body, (m0, l0))
    for g in range(group):
        out_t = acc_sc[g] / ls[g]
        o_ref[:, g * HEAD_DIM:(g + 1) * HEAD_DIM] = out_t.T.astype(o_ref.dtype)


MAX_SAFE_SCORE_BOUND = 60.0


def gqa_attention(q, k_all, v_all, n_kv_heads, score_bound):
    t, qw = q.shape
    tk_all = k_all.shape[0]
    group = qw // HEAD_DIM // n_kv_heads
    gw = group * HEAD_DIM
    tq = _tile(t, 512)
    tkc = _tile(tk_all, 1280)
    nch = tk_all // tkc
    k4 = k_all.reshape(nch, tkc, n_kv_heads, HEAD_DIM).transpose(2, 0, 1, 3)
    vt4 = v_all.reshape(nch, tkc, n_kv_heads, HEAD_DIM).transpose(2, 0, 3, 1)
    shift = score_bound.reshape(1, 1).astype(F32)

    def call(bounded):
        kern = functools.partial(_attn_kernel, group=group, n_chunks=nch, bounded=bounded)
        return pl.pallas_call(
            kern,
            grid=(n_kv_heads, t // tq),
            in_specs=[
                pl.BlockSpec(memory_space=pltpu.SMEM),
                pl.BlockSpec((tq, gw), lambda h, i: (i, h)),
                pl.BlockSpec((1, nch, tkc, HEAD_DIM), lambda h, i: (h, 0, 0, 0)),
                pl.BlockSpec((1, nch, HEAD_DIM, tkc), lambda h, i: (h, 0, 0, 0)),
            ],
            out_specs=pl.BlockSpec((tq, gw), lambda h, i: (i, h)),
            out_shape=jax.ShapeDtypeStruct((t, qw), BF16),
            scratch_shapes=[pltpu.VMEM((group, HEAD_DIM, tq), F32)],
            compiler_params=_params(("parallel", "parallel"), 48),
            name="gqa_flash_attention_bounded" if bounded else "gqa_flash_attention_online",
        )(shift, q, k4, vt4)

    return lax.cond(score_bound < MAX_SAFE_SCORE_BOUND, lambda: call(True), lambda: call(False))


def _conv_kernel(a_ref, g_ref, ap_ref, gp_ref, an_ref, gn_ref, w_ref, b_ref, lg_ref, lb_ref,
                 o_ref, y_sc, c_sc, sh_sc, *, ksize, rows, lanes):
    i = pl.program_id(0)
    n = pl.num_programs(0)
    tt, cw = o_ref.shape

    def glu(a, g):
        a = a.astype(F32)
        g = g.astype(F32)
        return a * jax.nn.sigmoid(g)

    y_sc[HALO:HALO + tt, :] = glu(a_ref[...], g_ref[...])
    y_sc[0:HALO, :] = jnp.where(i > 0, glu(ap_ref[...], gp_ref[...]), 0.0)
    y_sc[HALO + tt:HALO + tt + HALO, :] = jnp.where(i < n - 1, glu(an_ref[...], gn_ref[...]), 0.0)

    base = HALO - ksize // 2
    span = sh_sc.shape[1]
    for c0 in range(0, cw, lanes):
        wc = w_ref[:, c0:c0 + lanes]
        bc = b_ref[:, c0:c0 + lanes]
        for s in range(SUBLANES):
            sh_sc[s] = y_sc[s:s + span, c0:c0 + lanes]
        for r0 in range(0, tt, rows):
            acc = jnp.broadcast_to(bc, (rows, lanes))
            for j in range(ksize):
                s = (base + j) % SUBLANES
                top = r0 + base + j - s
                acc = acc + wc[j:j + 1, :] * sh_sc[s, top:top + rows, :]
            c_sc[r0:r0 + rows, c0:c0 + lanes] = acc

    y = c_sc[...]
    mu = jnp.mean(y, axis=-1, keepdims=True)
    yc = y - mu
    var = jnp.mean(yc * yc, axis=-1, keepdims=True)
    z = yc * lax.rsqrt(var + EPS) * lg_ref[...] + lb_ref[...]
    o_ref[...] = (z * jax.nn.sigmoid(z)).astype(o_ref.dtype)


def conformer_conv_pre(u, dw_w, dw_b, ln_g, ln_b):
    t, c2 = u.shape
    cw = c2 // 2
    ksize = dw_w.shape[0]
    assert ksize // 2 <= HALO - 1
    tt = _tile(t, 256, HALO)
    hb = tt // HALO
    n_hb = t // HALO
    cur_a = pl.BlockSpec((tt, cw), lambda i: (i, 0))
    cur_g = pl.BlockSpec((tt, cw), lambda i: (i, 1))
    prev_a = pl.BlockSpec((HALO, cw), lambda i: (jnp.maximum(i * hb - 1, 0), 0))
    prev_g = pl.BlockSpec((HALO, cw), lambda i: (jnp.maximum(i * hb - 1, 0), 1))
    next_a = pl.BlockSpec((HALO, cw), lambda i: (jnp.minimum((i + 1) * hb, n_hb - 1), 0))
    next_g = pl.BlockSpec((HALO, cw), lambda i: (jnp.minimum((i + 1) * hb, n_hb - 1), 1))
    vec = pl.BlockSpec((1, cw), lambda i: (0, 0))
    lanes = min(512, cw)
    kern = functools.partial(_conv_kernel, ksize=ksize, rows=min(64, tt), lanes=lanes)
    return pl.pallas_call(
        kern,
        grid=(t // tt,),
        in_specs=[cur_a, cur_g, prev_a, prev_g, next_a, next_g,
                  pl.BlockSpec((ksize, cw), lambda i: (0, 0)), vec, vec, vec],
        out_specs=pl.BlockSpec((tt, cw), lambda i: (i, 0)),
        out_shape=jax.ShapeDtypeStruct((t, cw), BF16),
        scratch_shapes=[pltpu.VMEM((tt + 2 * HALO, cw), F32), pltpu.VMEM((tt, cw), F32),
                        pltpu.VMEM((SUBLANES, tt + 2 * HALO - SUBLANES, lanes), F32)],
        compiler_params=_params(("parallel",), 40),
        name="conformer_conv",
    )(u, u, u, u, u, u, dw_w, dw_b, ln_g, ln_b)


def _merge_kernel(a_ref, wa_ref, b_ref, wb_ref, ga_ref, gb_ref, o_ref):
    pa = jnp.dot(a_ref[...], wa_ref[...], preferred_element_type=F32)
    pb = jnp.dot(b_ref[...], wb_ref[...], preferred_element_type=F32)
    ga = jax.nn.sigmoid(ga_ref[...].astype(F32))
    gb = jax.nn.sigmoid(gb_ref[...].astype(F32))
    o_ref[...] = (ga * pa + gb * pb).astype(o_ref.dtype)


def gated_merge(attn, w_attn_o, conv, w_conv_o, gates):
    t, ka = attn.shape
    kb = conv.shape[1]
    d = w_attn_o.shape[1]
    tm = _tile(t, 512, 8)
    tn = _tile(d, 1024)
    nb = d // tn
    return pl.pallas_call(
        _merge_kernel,
        grid=(nb, t // tm),
        in_specs=[
            pl.BlockSpec((tm, ka), lambda j, i: (i, 0)),
            pl.BlockSpec((ka, tn), lambda j, i: (0, j)),
            pl.BlockSpec((tm, kb), lambda j, i: (i, 0)),
            pl.BlockSpec((kb, tn), lambda j, i: (0, j)),
            pl.BlockSpec((tm, tn), lambda j, i: (i, j)),
            pl.BlockSpec((tm, tn), lambda j, i: (i, nb + j)),
        ],
        out_specs=pl.BlockSpec((tm, tn), lambda j, i: (i, j)),
        out_shape=jax.ShapeDtypeStruct((t, d), BF16),
        compiler_params=_params(("parallel", "parallel"), 48),
        name="gated_merge",
    )(attn, w_attn_o, conv, w_conv_o, gates, gates)


def _mm_resid_kernel(a_ref, w_ref, x_ref, g_ref, o_ref):
    acc = jnp.dot(a_ref[...], w_ref[...], preferred_element_type=F32)
    o_ref[...] = x_ref[...] + g_ref[...] * acc


def matmul_gated_residual(a, w, x, gate):
    m, k = a.shape
    n = w.shape[1]
    tm = _tile(m, 1024, 8)
    tn = _tile(n, 1024)
    return pl.pallas_call(
        _mm_resid_kernel,
        grid=(n // tn, m // tm),
        in_specs=[
            pl.BlockSpec((tm, k), lambda j, i: (i, 0)),
            pl.BlockSpec((k, tn), lambda j, i: (0, j)),
            pl.BlockSpec((tm, tn), lambda j, i: (i, j)),
            pl.BlockSpec((1, tn), lambda j, i: (0, j)),
        ],
        out_specs=pl.BlockSpec((tm, tn), lambda j, i: (i, j)),
        out_shape=jax.ShapeDtypeStruct((m, n), F32),
        compiler_params=_params(("parallel", "parallel"), 56),
        name="out_proj_residual",
    )(a, w, x, gate)


HI_HALF_MASK = -65536


def _pack_bf16_pair(lo, hi):
    lo_b = lax.bitcast_convert_type(lo.astype(BF16).astype(F32), jnp.int32)
    hi_b = lax.bitcast_convert_type(hi.astype(BF16).astype(F32), jnp.int32)
    return (hi_b & HI_HALF_MASK) | lax.shift_right_logical(lo_b, jnp.full(lo_b.shape, 16, jnp.int32))


def _unpack_bf16_pair(w):
    lo = lax.bitcast_convert_type(w << 16, F32)
    hi = lax.bitcast_convert_type(w & HI_HALF_MASK, F32)
    return lo, hi


def _router_kernel(x_ref, g_ref, sc_ref, sh_ref, wh_ref, wl_ref, h_ref, aff_ref, *, n_experts):
    x = x_ref[...]
    half = x.shape[1] // 2
    ms = jnp.mean(x * x, axis=-1, keepdims=True)
    h = x * lax.rsqrt(ms + EPS) * g_ref[...]
    h = h * (1.0 + sc_ref[...]) + sh_ref[...]
    hb = h.astype(BF16)
    h_ref[...] = _pack_bf16_pair(h[:, :half], h[:, half:])
    hl = (h - hb.astype(F32)).astype(BF16)
    logits = (jnp.dot(hb, wh_ref[...], preferred_element_type=F32)
              + jnp.dot(hb, wl_ref[...], preferred_element_type=F32)
              + jnp.dot(hl, wh_ref[...], preferred_element_type=F32))
    lane = lax.broadcasted_iota(jnp.int32, logits.shape, 1)
    logits = jnp.where(lane < n_experts, logits, -jnp.inf)
    mx = jnp.max(logits, axis=-1, keepdims=True)
    e = jnp.exp(logits - mx)
    aff_ref[...] = e / jnp.sum(e, axis=-1, keepdims=True)


def norm_router(x, g, scale, shift, w_router):
    t, d = x.shape
    n_experts = w_router.shape[1]
    wp = jnp.zeros((d, LANE), F32).at[:, :n_experts].set(w_router)
    wh = wp.astype(BF16)
    wl = (wp - wh.astype(F32)).astype(BF16)
    tm = _tile(t, 256, 8)
    vec = pl.BlockSpec((1, d), lambda i: (0, 0))
    wspec = pl.BlockSpec((d, LANE), lambda i: (0, 0))
    kern = functools.partial(_router_kernel, n_experts=n_experts)
    return pl.pallas_call(
        kern,
        grid=(t // tm,),
        in_specs=[pl.BlockSpec((tm, d), lambda i: (i, 0)), vec, vec, vec, wspec, wspec],
        out_specs=[pl.BlockSpec((tm, d // 2), lambda i: (i, 0)), pl.BlockSpec((tm, LANE), lambda i: (i, 0))],
        out_shape=[jax.ShapeDtypeStruct((t, d // 2), jnp.int32), jax.ShapeDtypeStruct((t, LANE), F32)],
        compiler_params=_params(("parallel",), 40),
        name="norm_router",
    )(x, g, scale, shift, wh, wl)


def _block_cumsum(m):
    nb = m.shape[0]
    r = lax.broadcasted_iota(jnp.int32, (LANE, LANE), 0)
    c = lax.broadcasted_iota(jnp.int32, (LANE, LANE), 1)
    upper = jnp.where(r <= c, 1.0, 0.0).astype(BF16)
    inblock = jnp.dot(m.astype(BF16), upper, preferred_element_type=F32)
    tot = jnp.broadcast_to(inblock[:, LANE - 1:LANE], (nb, LANE))
    rb = lax.broadcasted_iota(jnp.int32, (nb, nb), 0)
    cb = lax.broadcasted_iota(jnp.int32, (nb, nb), 1)
    lower = jnp.where(cb < rb, 1.0, 0.0).astype(BF16)
    before = jnp.dot(lower, tot.astype(BF16), preferred_element_type=F32)
    return inblock, before


def _select_kernel(a_ref, idx_ref, val_ref, soff_ref, *, cap):
    a = a_ref[0]
    nb = a.shape[0]
    bits = lax.bitcast_convert_type(a, jnp.int32)
    capf = jnp.float32(cap)

    def count(pred):
        cnt = jnp.sum(jnp.where(pred, 1.0, 0.0), axis=0, keepdims=True)
        return jnp.sum(cnt, axis=1, keepdims=True)

    def bisect(i, thr):
        cand = thr | lax.shift_left(jnp.int32(1), 30 - i)
        return jnp.where(count(bits >= cand) >= capf, cand, thr)

    thr = lax.fori_loop(0, 31, bisect, jnp.zeros((1, 1), jnp.int32))
    gt = bits > thr
    eq = bits == thr
    need = capf - count(gt)
    eqf = jnp.where(eq, 1.0, 0.0)
    eq_in, eq_before = _block_cumsum(eqf)
    take_eq = eq & ((eq_in + eq_before - eqf) < need)
    sel = jnp.where(gt | take_eq, 1.0, 0.0)
    inblock, before = _block_cumsum(sel)

    slot = lax.broadcasted_iota(jnp.int32, (1, cap), 1).astype(F32)
    upto = before[:, 0:1] + inblock[:, LANE - 1:LANE]
    blk = jnp.sum(jnp.where(upto <= slot, 1.0, 0.0), axis=0, keepdims=True)
    bi = lax.broadcasted_iota(jnp.int32, (nb, cap), 0).astype(F32)
    onehot = jnp.where(bi == blk, 1.0, 0.0).astype(BF16)

    rows = jnp.dot(inblock.T.astype(BF16), onehot, preferred_element_type=F32)
    before_t = before.T
    hi = jnp.floor(before_t * (1.0 / LANE))
    lo = before_t - hi * LANE
    start = (jnp.dot(hi.astype(BF16), onehot, preferred_element_type=F32) * LANE
             + jnp.dot(lo.astype(BF16), onehot, preferred_element_type=F32))
    rank = slot - start
    lane_of = jnp.sum(jnp.where(rows <= rank, 1.0, 0.0), axis=0, keepdims=True)
    idx_ref[0] = (blk * LANE + lane_of).astype(jnp.int32)

    at = a.T
    a1 = at.astype(BF16)
    r1 = at - a1.astype(F32)
    a2 = r1.astype(BF16)
    a3 = (r1 - a2.astype(F32)).astype(BF16)
    picked = (jnp.dot(a1, onehot, preferred_element_type=F32)
              + jnp.dot(a2, onehot, preferred_element_type=F32)
              + jnp.dot(a3, onehot, preferred_element_type=F32))
    li = lax.broadcasted_iota(jnp.int32, (LANE, cap), 0).astype(F32)
    val_ref[0] = jnp.sum(jnp.where(li == lane_of, picked, 0.0), axis=0, keepdims=True)
    soff_ref[0] = before_t[0:1, :].astype(jnp.int32)


def expert_select(aff3, cap):
    e, nb, _ = aff3.shape
    kern = functools.partial(_select_kernel, cap=cap)
    return pl.pallas_call(
        kern,
        grid=(e,),
        in_specs=[pl.BlockSpec((1, nb, LANE), lambda i: (i, 0, 0))],
        out_specs=[pl.BlockSpec((1, 1, cap), lambda i: (i, 0, 0)),
                   pl.BlockSpec((1, 1, cap), lambda i: (i, 0, 0)),
                   pl.BlockSpec((1, 1, nb), lambda i: (i, 0, 0))],
        out_shape=[jax.ShapeDtypeStruct((e, 1, cap), jnp.int32),
                   jax.ShapeDtypeStruct((e, 1, cap), F32),
                   jax.ShapeDtypeStruct((e, 1, nb), jnp.int32)],
        compiler_params=_params(("parallel",), 32),
        name="expert_select",
    )(aff3)


def _expert_up_kernel(idx_ref, h_hbm, wg_ref, wu_ref, o_ref, stage, xlo, xhi, sem, *, cap, phase_rows):
    e = pl.program_id(0)
    j = pl.program_id(1)
    half = xlo.shape[1]
    n_phases = cap // phase_rows

    def row_copy(p, r):
        tok = idx_ref[e * cap + p * phase_rows + r]
        return pltpu.make_async_copy(h_hbm.at[pl.ds(tok, 1), :],
                                     stage.at[p % 2, pl.ds(r, 1), :], sem.at[p % 2])

    def start_phase(p):
        def go(r, carry):
            row_copy(p, r).start()
            return carry
        lax.fori_loop(0, phase_rows, go, 0)

    def wait_phase(p):
        def go(r, carry):
            row_copy(p, r).wait()
            return carry
        lax.fori_loop(0, phase_rows, go, 0)

    @pl.when(j == 0)
    def _():
        start_phase(0)
        for p in range(n_phases):
            if p + 1 < n_phases:
                start_phase(p + 1)
            wait_phase(p)
            lo, hi = _unpack_bf16_pair(stage[p % 2])
            xlo[p * phase_rows:(p + 1) * phase_rows, :] = lo.astype(BF16)
            xhi[p * phase_rows:(p + 1) * phase_rows, :] = hi.astype(BF16)

    wg = wg_ref[0].astype(BF16)
    wu = wu_ref[0].astype(BF16)
    x_lo, x_hi = xlo[...], xhi[...]
    g = (jnp.dot(x_lo, wg[:half], preferred_element_type=F32)
         + jnp.dot(x_hi, wg[half:], preferred_element_type=F32))
    u = (jnp.dot(x_lo, wu[:half], preferred_element_type=F32)
         + jnp.dot(x_hi, wu[half:], preferred_element_type=F32))
    o_ref[0] = (g * jax.nn.sigmoid(g) * u).astype(o_ref.dtype)


def expert_up(idx_flat, h_packed, w_gate, w_up, cap):
    e, d, f = w_gate.shape
    half = d // 2
    tf = _tile(f, 256)
    phase_rows = _tile(cap, 512, 8)
    wspec = pl.BlockSpec((1, d, tf), lambda i, j, idx: (i, 0, j))
    kern = functools.partial(_expert_up_kernel, cap=cap, phase_rows=phase_rows)
    return pl.pallas_call(
        kern,
        grid_spec=pltpu.PrefetchScalarGridSpec(
            num_scalar_prefetch=1,
            grid=(e, f // tf),
            in_specs=[pl.BlockSpec(memory_space=pl.ANY), wspec, wspec],
            out_specs=pl.BlockSpec((1, cap, tf), lambda i, j, idx: (i, 0, j)),
            scratch_shapes=[pltpu.VMEM((2, phase_rows, half), jnp.int32),
                            pltpu.VMEM((cap, half), BF16),
                            pltpu.VMEM((cap, half), BF16),
                            pltpu.SemaphoreType.DMA((2,))],
        ),
        out_shape=jax.ShapeDtypeStruct((e, cap, f), BF16),
        compiler_params=_params(("arbitrary", "arbitrary"), 56),
        name="expert_up_gather",
    )(idx_flat, h_packed, w_gate, w_up)


def _expert_down_kernel(h_ref, wa_ref, wb_ref, v_ref, o_ref):
    h = h_ref[0]
    v = v_ref[0]
    ya = jnp.dot(h, wa_ref[0].astype(BF16), preferred_element_type=F32) * v
    yb = jnp.dot(h, wb_ref[0].astype(BF16), preferred_element_type=F32) * v
    o_ref[0] = _pack_bf16_pair(ya, yb)


def expert_down(hid, w_down, vals):
    e, c, f = hid.shape
    d = w_down.shape[2]
    half = d // 2
    tn = _tile(half, 256)
    nb = half // tn
    return pl.pallas_call(
        _expert_down_kernel,
        grid=(e, nb),
        in_specs=[pl.BlockSpec((1, c, f), lambda i, j: (i, 0, 0)),
                  pl.BlockSpec((1, f, tn), lambda i, j: (i, 0, j)),
                  pl.BlockSpec((1, f, tn), lambda i, j: (i, 0, nb + j)),
                  pl.BlockSpec((1, c, 1), lambda i, j: (i, 0, 0))],
        out_specs=pl.BlockSpec((1, c, tn), lambda i, j: (i, 0, j)),
        out_shape=jax.ShapeDtypeStruct((e, c, half), jnp.int32),
        compiler_params=_params(("parallel", "arbitrary"), 48),
        name="expert_down",
    )(hid, w_down, w_down, vals)


def _combine_kernel(idx_ref, soff_ref, x_ref, y_hbm, g2_ref, fg_ref, o_ref, stage, sem,
                    *, cap, n_blocks, n_experts):
    b = pl.program_id(0)
    tt, d = x_ref.shape
    half = d // 2
    stage[...] = jnp.zeros_like(stage)

    def slot_range(e):
        lo = soff_ref[e * n_blocks + b]
        nxt = soff_ref[e * n_blocks + jnp.minimum(b + 1, n_blocks - 1)]
        return lo, jnp.where(b + 1 < n_blocks, nxt, cap)

    def row_copy(e, j):
        tok = idx_ref[e * cap + j]
        return pltpu.make_async_copy(y_hbm.at[e, pl.ds(j, 1), :],
                                     stage.at[e, pl.ds(tok - b * tt, 1), :], sem.at[0])

    for e in range(n_experts):
        lo, hi = slot_range(e)

        def go(j, carry, e=e):
            row_copy(e, j).start()
            return carry
        lax.fori_loop(lo, hi, go, 0)
    for e in range(n_experts):
        lo, hi = slot_range(e)

        def done(j, carry, e=e):
            row_copy(e, j).wait()
            return carry
        lax.fori_loop(lo, hi, done, 0)

    f_lo = jnp.zeros((tt, half), F32)
    f_hi = jnp.zeros((tt, half), F32)
    for e in range(n_experts):
        lo, hi = _unpack_bf16_pair(stage[e])
        f_lo = f_lo + lo
        f_hi = f_hi + hi
    x = x_ref[...]
    x_lo = x[:, :half] + g2_ref[:, :half] * f_lo
    x_hi = x[:, half:] + g2_ref[:, half:] * f_hi
    ms = (jnp.sum(x_lo * x_lo, axis=-1, keepdims=True)
          + jnp.sum(x_hi * x_hi, axis=-1, keepdims=True)) * (1.0 / d)
    inv = lax.rsqrt(ms + EPS)
    o_ref[:, :half] = x_lo * inv * fg_ref[:, :half]
    o_ref[:, half:] = x_hi * inv * fg_ref[:, half:]


def combine_final_norm(idx_flat, soff_flat, x, y_packed, gate, final_g, cap):
    t, d = x.shape
    n_experts = y_packed.shape[0]
    tt = LANE
    n_blocks = t // tt
    vec = pl.BlockSpec((1, d), lambda i, a, b: (0, 0))
    row = pl.BlockSpec((tt, d), lambda i, a, b: (i, 0))
    kern = functools.partial(_combine_kernel, cap=cap, n_blocks=n_blocks, n_experts=n_experts)
    return pl.pallas_call(
        kern,
        grid_spec=pltpu.PrefetchScalarGridSpec(
            num_scalar_prefetch=2,
            grid=(n_blocks,),
            in_specs=[row, pl.BlockSpec(memory_space=pl.ANY), vec, vec],
            out_specs=row,
            scratch_shapes=[pltpu.VMEM((n_experts, tt, d // 2), jnp.int32),
                            pltpu.SemaphoreType.DMA((1,))],
        ),
        out_shape=jax.ShapeDtypeStruct((t, d), F32),
        compiler_params=_params(("arbitrary",), 48),
        name="combine_final_norm",
    )(idx_flat, soff_flat, x, y_packed, gate, final_g)


def kernel(x, c, ctx, c_ctx, w_ada, b_ada, norm1_g, w_in, b_in, q_norm_g, k_norm_g, w_attn_o,
           dw_w, dw_b, conv_ln_g, conv_ln_b, w_conv_o, w_out, norm2_g, w_router, w_gate, w_up,
           w_down, final_g):
    batch, t, d = x.shape
    depth = w_ada.shape[0]
    assert batch == 1 and depth == 1
    q_w = w_attn_o.shape[1]
    conv_w = w_conv_o.shape[1]
    in_w = w_in.shape[2]
    kv_w = (in_w - q_w - 2 * conv_w - 2 * d) // 2
    n_kv = kv_w // HEAD_DIM
    n_experts = w_router.shape[2]
    cap = CAPACITY_FACTOR * t // n_experts
    o1, o2, o3, o4 = q_w, q_w + kv_w, q_w + 2 * kv_w, q_w + 2 * kv_w + 2 * conv_w

    row = lambda v: v.reshape(1, -1)
    x2, ctx2 = x[0], ctx[0]

    cvecs_t = jnp.stack([c[0], c_ctx], axis=1)
    mod = ada_modulation(cvecs_t, w_ada[0], row(b_ada[0]))
    sh1, sc1, g1, sh2, sc2, g2 = [mod[0:1, i * d:(i + 1) * d] for i in range(N_MOD)]
    csh1, csc1 = mod[1:2, 0:d], mod[1:2, d:2 * d]

    h_lat = norm_modulate(x2, row(norm1_g[0]), sc1, sh1)
    h_ctx = norm_modulate(ctx2, row(norm1_g[0]), csc1, csh1)

    w_in0, b_in0 = w_in[0], row(b_in[0])
    w_qkv = w_in0[:, :o3].astype(BF16)
    qkv = matmul_bias(h_lat, w_qkv, b_in0[:, :o3])
    u = matmul_bias(h_lat, w_in0[:, o3:o4].astype(BF16), b_in0[:, o3:o4])
    gates = matmul_bias(h_lat, w_in0[:, o4:].astype(BF16), b_in0[:, o4:])
    kv_ctx = matmul_bias(h_ctx, w_qkv[:, o1:o3], b_in0[:, o1:o3])

    cosf, sins = rope_tables(t)
    qg, kg = row(q_norm_g[0]), row(k_norm_g[0])
    gw = q_w // n_kv
    q_scale = HEAD_DIM ** -0.5 * LOG2E
    q_r = head_norm(qkv, 0, n_kv, gw, qg, cosf, sins, rope=True, scale=q_scale)
    k_r = head_norm(qkv, q_w // kv_w, 1, kv_w, kg, cosf, sins, rope=True, scale=1.0)
    tc = ctx2.shape[0]
    k_ctx = head_norm(kv_ctx, 0, 1, kv_w, kg, cosf[:tc], sins[:tc], rope=False, scale=1.0)
    k_all = jnp.concatenate([k_ctx, k_r], axis=0)
    v_all = jnp.concatenate([kv_ctx[:, kv_w:], qkv[:, o2:o3]], axis=0)
    score_bound = (1.02 * HEAD_DIM * q_scale) * jnp.max(jnp.abs(q_norm_g[0])) * jnp.max(jnp.abs(k_norm_g[0]))
    attn = gqa_attention(q_r, k_all, v_all, n_kv, score_bound)

    conv = conformer_conv_pre(u, dw_w[0], row(dw_b[0]), row(conv_ln_g[0]), row(conv_ln_b[0]))
    merged = gated_merge(attn, w_attn_o[0].astype(BF16), conv, w_conv_o[0].astype(BF16), gates)
    x1 = matmul_gated_residual(merged, w_out[0].astype(BF16), x2, g1)

    h2p, aff = norm_router(x1, row(norm2_g[0]), sc2, sh2, w_router[0])
    aff3 = aff[:, :n_experts].T.reshape(n_experts, t // LANE, LANE)
    idx, vals, soff = expert_select(aff3, cap)
    idx_flat = idx.reshape(-1)
    hid = expert_up(idx_flat, h2p, w_gate[0], w_up[0], cap)
    y = expert_down(hid, w_down[0], vals.reshape(n_experts, cap, 1))
    out = combine_final_norm(idx_flat, soff.reshape(-1), x1, y, g2, row(final_g), cap)
    return out[None]
```

```python
import functools

import jax
import jax.numpy as jnp
from jax import lax
from jax.experimental import pallas as pl
from jax.experimental.pallas import tpu as pltpu

F32 = jnp.float32
BF16 = jnp.bfloat16

HEAD_DIM = 128
GRID_W = 64
ROPE_THETA = 10000.0
ROPE_AXIS_DIM = HEAD_DIM // 2
EPS = 1e-6
CAPACITY_FACTOR = 2
N_MOD = 6
HALO = 16
LANE = 128
SUBLANES = 8
MIB = 1024 * 1024
LOG2E = 1.4426950408889634


def _tile(n, pref, mult=LANE):
    if n <= pref:
        return n
    t = (pref // mult) * mult
    while t >= mult:
        if n % t == 0:
            return t
        t -= mult
    return n


def _params(sem, vmem_mib):
    return pltpu.CompilerParams(dimension_semantics=sem, vmem_limit_bytes=vmem_mib * MIB)


def _ada_kernel(s_ref, w_ref, b_ref, o_ref):
    k = pl.program_id(1)

    @pl.when(k == 0)
    def _():
        o_ref[...] = jnp.broadcast_to(b_ref[...], o_ref.shape)

    s = s_ref[...]
    s = s * jax.nn.sigmoid(s)
    w = w_ref[...]
    r0 = jnp.sum(s[:, 0:1] * w, axis=0, keepdims=True)
    r1 = jnp.sum(s[:, 1:2] * w, axis=0, keepdims=True)
    o_ref[0:1, :] += r0
    o_ref[1:2, :] += r1


def ada_modulation(cvecs_t, w_ada, b_ada):
    d, n = w_ada.shape
    tk = _tile(d, 1024, 8)
    tn = _tile(n, 2048)
    return pl.pallas_call(
        _ada_kernel,
        grid=(n // tn, d // tk),
        in_specs=[
            pl.BlockSpec((tk, 2), lambda j, k: (k, 0)),
            pl.BlockSpec((tk, tn), lambda j, k: (k, j)),
            pl.BlockSpec((1, tn), lambda j, k: (0, j)),
        ],
        out_specs=pl.BlockSpec((2, tn), lambda j, k: (0, j)),
        out_shape=jax.ShapeDtypeStruct((2, n), F32),
        compiler_params=_params(("parallel", "arbitrary"), 48),
        name="ada_modulation",
    )(cvecs_t, w_ada, b_ada)


def _normmod_kernel(x_ref, g_ref, sc_ref, sh_ref, o_ref):
    x = x_ref[...].astype(F32)
    ms = jnp.mean(x * x, axis=-1, keepdims=True)
    y = x * lax.rsqrt(ms + EPS) * g_ref[...]
    o_ref[...] = (y * (1.0 + sc_ref[...]) + sh_ref[...]).astype(o_ref.dtype)


def norm_modulate(x, g, scale, shift, out_dtype=BF16):
    t, d = x.shape
    tm = _tile(t, 256, 8)
    vec = pl.BlockSpec((1, d), lambda i: (0, 0))
    return pl.pallas_call(
        _normmod_kernel,
        grid=(t // tm,),
        in_specs=[pl.BlockSpec((tm, d), lambda i: (i, 0)), vec, vec, vec],
        out_specs=pl.BlockSpec((tm, d), lambda i: (i, 0)),
        out_shape=jax.ShapeDtypeStruct((t, d), out_dtype),
        compiler_params=_params(("parallel",), 40),
        name="norm_modulate",
    )(x, g, scale, shift)


def _mm_bias_kernel(a_ref, w_ref, b_ref, o_ref):
    acc = jnp.dot(a_ref[...], w_ref[...], preferred_element_type=F32)
    o_ref[...] = (acc + b_ref[...]).astype(o_ref.dtype)


def matmul_bias(a, w, b, out_dtype=BF16, tm_pref=1024, tn_pref=1024):
    m, k = a.shape
    n = w.shape[1]
    tm = _tile(m, tm_pref, 8)
    tn = _tile(n, tn_pref)
    return pl.pallas_call(
        _mm_bias_kernel,
        grid=(n // tn, m // tm),
        in_specs=[
            pl.BlockSpec((tm, k), lambda j, i: (i, 0)),
            pl.BlockSpec((k, tn), lambda j, i: (0, j)),
            pl.BlockSpec((1, tn), lambda j, i: (0, j)),
        ],
        out_specs=pl.BlockSpec((tm, tn), lambda j, i: (i, j)),
        out_shape=jax.ShapeDtypeStruct((m, n), out_dtype),
        compiler_params=_params(("parallel", "parallel"), 56),
        name="matmul_bias",
    )(a, w, b)


def _norm_heads(x, g, o_ref, *, scale, cosf=None, sins=None):
    if cosf is not None:
        lane = lax.broadcasted_iota(jnp.int32, cosf.shape, 1)
        first_half = (lane % (ROPE_AXIS_DIM)) < (ROPE_AXIS_DIM // 2)
    for h in range(x.shape[1] // HEAD_DIM):
        xh = x[:, h * HEAD_DIM:(h + 1) * HEAD_DIM]
        ms = jnp.mean(xh * xh, axis=-1, keepdims=True)
        y = xh * lax.rsqrt(ms + EPS) * g
        if cosf is not None:
            up = pltpu.roll(y, HEAD_DIM - ROPE_AXIS_DIM // 2, axis=1)
            dn = pltpu.roll(y, ROPE_AXIS_DIM // 2, axis=1)
            y = y * cosf + jnp.where(first_half, up, dn) * sins
        if scale != 1.0:
            y = y * scale
        o_ref[:, h * HEAD_DIM:(h + 1) * HEAD_DIM] = y.astype(o_ref.dtype)


def _headnorm_kernel(x_ref, g_ref, cos_ref, sin_ref, o_ref, *, n_heads, rope, scale):
    x = x_ref[...].astype(F32)
    if rope:
        _norm_heads(x, g_ref[...], o_ref, scale=scale, cosf=cos_ref[...], sins=sin_ref[...])
    else:
        _norm_heads(x, g_ref[...], o_ref, scale=scale)


def _qkv_kernel(a_ref, w_ref, b_ref, qg_ref, kg_ref, cos_ref, sin_ref, o_ref, *, n_q_blocks, q_scale):
    j = pl.program_id(0)
    acc = jnp.dot(a_ref[...], w_ref[...], preferred_element_type=F32) + b_ref[...]

    @pl.when(j < n_q_blocks)
    def _():
        _norm_heads(acc, qg_ref[...], o_ref, scale=q_scale, cosf=cos_ref[...], sins=sin_ref[...])

    @pl.when(j == n_q_blocks)
    def _():
        _norm_heads(acc, kg_ref[...], o_ref, scale=1.0, cosf=cos_ref[...], sins=sin_ref[...])

    @pl.when(j > n_q_blocks)
    def _():
        o_ref[...] = acc.astype(o_ref.dtype)


def qkv_projection(a, w, b, qg, kg, cos, sin, *, block_w, n_q_blocks, q_scale):
    m, k = a.shape
    n = w.shape[1]
    assert n == (n_q_blocks + 2) * block_w
    tm = _tile(m, 1024, 8)
    kern = functools.partial(_qkv_kernel, n_q_blocks=n_q_blocks, q_scale=q_scale)
    gain = pl.BlockSpec((1, HEAD_DIM), lambda j, i: (0, 0))
    tab = pl.BlockSpec((tm, HEAD_DIM), lambda j, i: (i, 0))
    return pl.pallas_call(
        kern,
        grid=(n // block_w, m // tm),
        in_specs=[
            pl.BlockSpec((tm, k), lambda j, i: (i, 0)),
            pl.BlockSpec((k, block_w), lambda j, i: (0, j)),
            pl.BlockSpec((1, block_w), lambda j, i: (0, j)),
            gain, gain, tab, tab,
        ],
        out_specs=pl.BlockSpec((tm, block_w), lambda j, i: (i, j)),
        out_shape=jax.ShapeDtypeStruct((m, n), BF16),
        compiler_params=_params(("parallel", "parallel"), 56),
        name="qkv_projection",
    )(a, w, b, qg, kg, cos, sin)


def head_norm(x, col_block, n_blocks, width, g, cos, sin, *, rope, scale):
    t = x.shape[0]
    tm = _tile(t, 512, 8)
    n_heads = width // HEAD_DIM
    kern = functools.partial(_headnorm_kernel, n_heads=n_heads, rope=rope, scale=scale)
    tab = pl.BlockSpec((tm, HEAD_DIM), lambda i, j: (i, 0))
    return pl.pallas_call(
        kern,
        grid=(t // tm, n_blocks),
        in_specs=[
            pl.BlockSpec((tm, width), lambda i, j: (i, col_block + j)),
            pl.BlockSpec((1, HEAD_DIM), lambda i, j: (0, 0)),
            tab, tab,
        ],
        out_specs=pl.BlockSpec((tm, width), lambda i, j: (i, j)),
        out_shape=jax.ShapeDtypeStruct((t, n_blocks * width), BF16),
        compiler_params=_params(("parallel", "parallel"), 32),
        name="head_norm_rope",
    )(x, g, cos, sin)


def rope_tables(n_tokens):
    n_rows = n_tokens // GRID_W
    row = jnp.repeat(jnp.arange(n_rows, dtype=jnp.int32), GRID_W)
    col = jnp.tile(jnp.arange(GRID_W, dtype=jnp.int32), n_rows)
    pos = jnp.stack([row, col], axis=-1).astype(F32)
    inv_freq = ROPE_THETA ** (-jnp.arange(0, ROPE_AXIS_DIM, 2, dtype=F32) / ROPE_AXIS_DIM)
    ang = pos[:, :, None] * inv_freq
    cos, sin = jnp.cos(ang), jnp.sin(ang)
    cosf = jnp.concatenate([cos, cos], axis=-1).reshape(n_tokens, HEAD_DIM)
    sins = jnp.concatenate([-sin, sin], axis=-1).reshape(n_tokens, HEAD_DIM)
    return cosf, sins


def _attn_kernel(b_ref, q_ref, k_ref, vt_ref, o_ref, acc_sc, *, group, n_chunks, bounded):
    tq = q_ref.shape[0]
    acc_sc[...] = jnp.zeros_like(acc_sc)
    qs = [q_ref[:, g * HEAD_DIM:(g + 1) * HEAD_DIM] for g in range(group)]
    shift = b_ref[0, 0]

    def body(c, carry):
        ms, ls = carry
        kc = k_ref[0, c]
        vc = vt_ref[0, c]
        new_m, new_l = [], []
        for g in range(group):
            s = lax.dot_general(kc, qs[g], (((1,), (1,)), ((), ())),
                                preferred_element_type=F32)
            if bounded:
                p = jnp.exp2(s - shift)
                new_m.append(ms[g])
                new_l.append(ls[g] + jnp.sum(p, axis=0, keepdims=True))
                acc_sc[g] += jnp.dot(vc, p.astype(BF16), preferred_element_type=F32)
            else:
                m_new = jnp.maximum(ms[g], jnp.max(s, axis=0, keepdims=True))
                p = jnp.exp2(s - m_new)
                alpha = jnp.exp2(ms[g] - m_new)
                new_m.append(m_new)
                new_l.append(alpha * ls[g] + jnp.sum(p, axis=0, keepdims=True))
                pv = jnp.dot(vc, p.astype(BF16), preferred_element_type=F32)
                acc_sc[g] = alpha * acc_sc[g] + pv
        return tuple(new_m), tuple(new_l)

    m0 = tuple(jnp.full((1, tq), -jnp.inf, F32) for _ in range(group))
    l0 = tuple(jnp.zeros((1, tq), F32) for _ in range(group))
    _, ls = lax.fori_loop(0, n_chunks, body, (m0, l0))
    for g in range(group):
        out_t = acc_sc[g] / ls[g]
        o_ref[:, g * HEAD_DIM:(g + 1) * HEAD_DIM] = out_t.T.astype(o_ref.dtype)


MAX_SAFE_SCORE_BOUND = 60.0


def gqa_attention(q, qw, k_all, v_all, n_kv_heads, score_bound):
    t = q.shape[0]
    tk_all = k_all.shape[0]
    group = qw // HEAD_DIM // n_kv_heads
    gw = group * HEAD_DIM
    tq = _tile(t, 512)
    tkc = _tile(tk_all, 1280)
    nch = tk_all // tkc
    k4 = k_all.reshape(nch, tkc, n_kv_heads, HEAD_DIM).transpose(2, 0, 1, 3)
    vt4 = v_all.reshape(nch, tkc, n_kv_heads, HEAD_DIM).transpose(2, 0, 3, 1)
    shift = score_bound.reshape(1, 1).astype(F32)

    def call(bounded):
        kern = functools.partial(_attn_kernel, group=group, n_chunks=nch, bounded=bounded)
        return pl.pallas_call(
            kern,
            grid=(n_kv_heads, t // tq),
            in_specs=[
                pl.BlockSpec(memory_space=pltpu.SMEM),
                pl.BlockSpec((tq, gw), lambda h, i: (i, h)),
                pl.BlockSpec((1, nch, tkc, HEAD_DIM), lambda h, i: (h, 0, 0, 0)),
                pl.BlockSpec((1, nch, HEAD_DIM, tkc), lambda h, i: (h, 0, 0, 0)),
            ],
            out_specs=pl.BlockSpec((tq, gw), lambda h, i: (i, h)),
            out_shape=jax.ShapeDtypeStruct((t, qw), BF16),
            scratch_shapes=[pltpu.VMEM((group, HEAD_DIM, tq), F32)],
            compiler_params=_params(("parallel", "parallel"), 48),
            name="gqa_flash_attention_bounded" if bounded else "gqa_flash_attention_online",
        )(shift, q, k4, vt4)

    return lax.cond(score_bound < MAX_SAFE_SCORE_BOUND, lambda: call(True), lambda: call(False))


def _conv_kernel(a_ref, g_ref, ap_ref, gp_ref, an_ref, gn_ref, w_ref, b_ref, lg_ref, lb_ref,
                 o_ref, y_sc, c_sc, sh_sc, *, ksize, rows, lanes):
    i = pl.program_id(0)
    n = pl.num_programs(0)
    tt, cw = o_ref.shape

    def glu(a, g):
        a = a.astype(F32)
        g = g.astype(F32)
        return a * jax.nn.sigmoid(g)

    y_sc[HALO:HALO + tt, :] = glu(a_ref[...], g_ref[...])
    y_sc[0:HALO, :] = jnp.where(i > 0, glu(ap_ref[...], gp_ref[...]), 0.0)
    y_sc[HALO + tt:HALO + tt + HALO, :] = jnp.where(i < n - 1, glu(an_ref[...], gn_ref[...]), 0.0)

    base = HALO - ksize // 2
    span = sh_sc.shape[1]
    for c0 in range(0, cw, lanes):
        wc = w_ref[:, c0:c0 + lanes]
        bc = b_ref[:, c0:c0 + lanes]
        for s in range(SUBLANES):
            sh_sc[s] = y_sc[s:s + span, c0:c0 + lanes]
        for r0 in range(0, tt, rows):
            acc = jnp.broadcast_to(bc, (rows, lanes))
            for j in range(ksize):
                s = (base + j) % SUBLANES
                top = r0 + base + j - s
                acc = acc + wc[j:j + 1, :] * sh_sc[s, top:top + rows, :]
            c_sc[r0:r0 + rows, c0:c0 + lanes] = acc

    y = c_sc[...]
    mu = jnp.mean(y, axis=-1, keepdims=True)
    yc = y - mu
    var = jnp.mean(yc * yc, axis=-1, keepdims=True)
    z = yc * lax.rsqrt(var + EPS) * lg_ref[...] + lb_ref[...]
    o_ref[...] = (z * jax.nn.sigmoid(z)).astype(o_ref.dtype)


def conformer_conv_pre(u, dw_w, dw_b, ln_g, ln_b):
    t, c2 = u.shape
    cw = c2 // 2
    ksize = dw_w.shape[0]
    assert ksize // 2 <= HALO - 1
    tt = _tile(t, 256, HALO)
    hb = tt // HALO
    n_hb = t // HALO
    cur_a = pl.BlockSpec((tt, cw), lambda i: (i, 0))
    cur_g = pl.BlockSpec((tt, cw), lambda i: (i, 1))
    prev_a = pl.BlockSpec((HALO, cw), lambda i: (jnp.maximum(i * hb - 1, 0), 0))
    prev_g = pl.BlockSpec((HALO, cw), lambda i: (jnp.maximum(i * hb - 1, 0), 1))
    next_a = pl.BlockSpec((HALO, cw), lambda i: (jnp.minimum((i + 1) * hb, n_hb - 1), 0))
    next_g = pl.BlockSpec((HALO, cw), lambda i: (jnp.minimum((i + 1) * hb, n_hb - 1), 1))
    vec = pl.BlockSpec((1, cw), lambda i: (0, 0))
    lanes = min(512, cw)
    kern = functools.partial(_conv_kernel, ksize=ksize, rows=min(64, tt), lanes=lanes)
    return pl.pallas_call(
        kern,
        grid=(t // tt,),
        in_specs=[cur_a, cur_g, prev_a, prev_g, next_a, next_g,
                  pl.BlockSpec((ksize, cw), lambda i: (0, 0)), vec, vec, vec],
        out_specs=pl.BlockSpec((tt, cw), lambda i: (i, 0)),
        out_shape=jax.ShapeDtypeStruct((t, cw), BF16),
        scratch_shapes=[pltpu.VMEM((tt + 2 * HALO, cw), F32), pltpu.VMEM((tt, cw), F32),
                        pltpu.VMEM((SUBLANES, tt + 2 * HALO - SUBLANES, lanes), F32)],
        compiler_params=_params(("parallel",), 40),
        name="conformer_conv",
    )(u, u, u, u, u, u, dw_w, dw_b, ln_g, ln_b)


def _merge_kernel(a_ref, wa_ref, b_ref, wb_ref, ga_ref, gb_ref, o_ref):
    pa = jnp.dot(a_ref[...], wa_ref[...], preferred_element_type=F32)
    pb = jnp.dot(b_ref[...], wb_ref[...], preferred_element_type=F32)
    ga = jax.nn.sigmoid(ga_ref[...].astype(F32))
    gb = jax.nn.sigmoid(gb_ref[...].astype(F32))
    o_ref[...] = (ga * pa + gb * pb).astype(o_ref.dtype)


def gated_merge(attn, w_attn_o, conv, w_conv_o, gates):
    t, ka = attn.shape
    kb = conv.shape[1]
    d = w_attn_o.shape[1]
    tm = _tile(t, 512, 8)
    tn = _tile(d, 1024)
    nb = d // tn
    return pl.pallas_call(
        _merge_kernel,
        grid=(nb, t // tm),
        in_specs=[
            pl.BlockSpec((tm, ka), lambda j, i: (i, 0)),
            pl.BlockSpec((ka, tn), lambda j, i: (0, j)),
            pl.BlockSpec((tm, kb), lambda j, i: (i, 0)),
            pl.BlockSpec((kb, tn), lambda j, i: (0, j)),
            pl.BlockSpec((tm, tn), lambda j, i: (i, j)),
            pl.BlockSpec((tm, tn), lambda j, i: (i, nb + j)),
        ],
        out_specs=pl.BlockSpec((tm, tn), lambda j, i: (i, j)),
        out_shape=jax.ShapeDtypeStruct((t, d), BF16),
        compiler_params=_params(("parallel", "parallel"), 48),
        name="gated_merge",
    )(attn, w_attn_o, conv, w_conv_o, gates, gates)


def _mm_resid_kernel(a_ref, w_ref, x_ref, g_ref, o_ref):
    acc = jnp.dot(a_ref[...], w_ref[...], preferred_element_type=F32)
    o_ref[...] = x_ref[...] + g_ref[...] * acc


def matmul_gated_residual(a, w, x, gate):
    m, k = a.shape
    n = w.shape[1]
    tm = _tile(m, 1024, 8)
    tn = _tile(n, 1024)
    return pl.pallas_call(
        _mm_resid_kernel,
        grid=(n // tn, m // tm),
        in_specs=[
            pl.BlockSpec((tm, k), lambda j, i: (i, 0)),
            pl.BlockSpec((k, tn), lambda j, i: (0, j)),
            pl.BlockSpec((tm, tn), lambda j, i: (i, j)),
            pl.BlockSpec((1, tn), lambda j, i: (0, j)),
        ],
        out_specs=pl.BlockSpec((tm, tn), lambda j, i: (i, j)),
        out_shape=jax.ShapeDtypeStruct((m, n), F32),
        compiler_params=_params(("parallel", "parallel"), 56),
        name="out_proj_residual",
    )(a, w, x, gate)


HI_HALF_MASK = -65536


def _pack_bf16_pair(lo, hi):
    lo_b = lax.bitcast_convert_type(lo.astype(BF16).astype(F32), jnp.int32)
    hi_b = lax.bitcast_convert_type(hi.astype(BF16).astype(F32), jnp.int32)
    return (hi_b & HI_HALF_MASK) | lax.shift_right_logical(lo_b, jnp.full(lo_b.shape, 16, jnp.int32))


def _unpack_bf16_pair(w):
    lo = lax.bitcast_convert_type(w << 16, F32)
    hi = lax.bitcast_convert_type(w & HI_HALF_MASK, F32)
    return lo, hi


def _router_kernel(x_ref, g_ref, sc_ref, sh_ref, wh_ref, wl_ref, h_ref, aff_ref, *, n_experts):
    x = x_ref[...]
    half = x.shape[1] // 2
    ms = jnp.mean(x * x, axis=-1, keepdims=True)
    h = x * lax.rsqrt(ms + EPS) * g_ref[...]
    h = h * (1.0 + sc_ref[...]) + sh_ref[...]
    hb = h.astype(BF16)
    h_ref[...] = _pack_bf16_pair(h[:, :half], h[:, half:])
    hl = (h - hb.astype(F32)).astype(BF16)
    logits = (jnp.dot(hb, wh_ref[...], preferred_element_type=F32)
              + jnp.dot(hb, wl_ref[...], preferred_element_type=F32)
              + jnp.dot(hl, wh_ref[...], preferred_element_type=F32))
    lane = lax.broadcasted_iota(jnp.int32, logits.shape, 1)
    logits = jnp.where(lane < n_experts, logits, -jnp.inf)
    mx = jnp.max(logits, axis=-1, keepdims=True)
    e = jnp.exp(logits - mx)
    aff_ref[...] = e / jnp.sum(e, axis=-1, keepdims=True)


def norm_router(x, g, scale, shift, w_router):
    t, d = x.shape
    n_experts = w_router.shape[1]
    wp = jnp.zeros((d, LANE), F32).at[:, :n_experts].set(w_router)
    wh = wp.astype(BF16)
    wl = (wp - wh.astype(F32)).astype(BF16)
    tm = _tile(t, 256, 8)
    vec = pl.BlockSpec((1, d), lambda i: (0, 0))
    wspec = pl.BlockSpec((d, LANE), lambda i: (0, 0))
    kern = functools.partial(_router_kernel, n_experts=n_experts)
    return pl.pallas_call(
        kern,
        grid=(t // tm,),
        in_specs=[pl.BlockSpec((tm, d), lambda i: (i, 0)), vec, vec, vec, wspec, wspec],
        out_specs=[pl.BlockSpec((tm, d // 2), lambda i: (i, 0)), pl.BlockSpec((tm, LANE), lambda i: (i, 0))],
        out_shape=[jax.ShapeDtypeStruct((t, d // 2), jnp.int32), jax.ShapeDtypeStruct((t, LANE), F32)],
        compiler_params=_params(("parallel",), 40),
        name="norm_router",
    )(x, g, scale, shift, wh, wl)


def _block_cumsum(m):
    nb = m.shape[0]
    r = lax.broadcasted_iota(jnp.int32, (LANE, LANE), 0)
    c = lax.broadcasted_iota(jnp.int32, (LANE, LANE), 1)
    upper = jnp.where(r <= c, 1.0, 0.0).astype(BF16)
    inblock = jnp.dot(m.astype(BF16), upper, preferred_element_type=F32)
    tot = jnp.broadcast_to(inblock[:, LANE - 1:LANE], (nb, LANE))
    rb = lax.broadcasted_iota(jnp.int32, (nb, nb), 0)
    cb = lax.broadcasted_iota(jnp.int32, (nb, nb), 1)
    lower = jnp.where(cb < rb, 1.0, 0.0).astype(BF16)
    before = jnp.dot(lower, tot.astype(BF16), preferred_element_type=F32)
    return inblock, before


def _select_kernel(a_ref, idx_ref, val_ref, soff_ref, *, cap):
    a = a_ref[0]
    nb = a.shape[0]
    bits = lax.bitcast_convert_type(a, jnp.int32)
    capf = jnp.float32(cap)

    def count(pred):
        cnt = jnp.sum(jnp.where(pred, 1.0, 0.0), axis=0, keepdims=True)
        return jnp.sum(cnt, axis=1, keepdims=True)

    def bisect(i, thr):
        cand = thr | lax.shift_left(jnp.int32(1), 30 - i)
        return jnp.where(count(bits >= cand) >= capf, cand, thr)

    thr = lax.fori_loop(0, 31, bisect, jnp.zeros((1, 1), jnp.int32))
    gt = bits > thr
    eq = bits == thr
    need = capf - count(gt)
    eqf = jnp.where(eq, 1.0, 0.0)
    eq_in, eq_before = _block_cumsum(eqf)
    take_eq = eq & ((eq_in + eq_before - eqf) < need)
    sel = jnp.where(gt | take_eq, 1.0, 0.0)
    inblock, before = _block_cumsum(sel)

    slot = lax.broadcasted_iota(jnp.int32, (1, cap), 1).astype(F32)
    upto = before[:, 0:1] + inblock[:, LANE - 1:LANE]
    blk = jnp.sum(jnp.where(upto <= slot, 1.0, 0.0), axis=0, keepdims=True)
    bi = lax.broadcasted_iota(jnp.int32, (nb, cap), 0).astype(F32)
    onehot = jnp.where(bi == blk, 1.0, 0.0).astype(BF16)

    rows = jnp.dot(inblock.T.astype(BF16), onehot, preferred_element_type=F32)
    before_t = before.T
    hi = jnp.floor(before_t * (1.0 / LANE))
    lo = before_t - hi * LANE
    start = (jnp.dot(hi.astype(BF16), onehot, preferred_element_type=F32) * LANE
             + jnp.dot(lo.astype(BF16), onehot, preferred_element_type=F32))
    rank = slot - start
    lane_of = jnp.sum(jnp.where(rows <= rank, 1.0, 0.0), axis=0, keepdims=True)
    idx_ref[0] = (blk * LANE + lane_of).astype(jnp.int32)

    at = a.T
    a1 = at.astype(BF16)
    r1 = at - a1.astype(F32)
    a2 = r1.astype(BF16)
    a3 = (r1 - a2.astype(F32)).astype(BF16)
    picked = (jnp.dot(a1, onehot, preferred_element_type=F32)
              + jnp.dot(a2, onehot, preferred_element_type=F32)
              + jnp.dot(a3, onehot, preferred_element_type=F32))
    li = lax.broadcasted_iota(jnp.int32, (LANE, cap), 0).astype(F32)
    val_ref[0] = jnp.sum(jnp.where(li == lane_of, picked, 0.0), axis=0, keepdims=True)
    soff_ref[0] = before_t[0:1, :].astype(jnp.int32)


def expert_select(aff3, cap):
    e, nb, _ = aff3.shape
    kern = functools.partial(_select_kernel, cap=cap)
    return pl.pallas_call(
        kern,
        grid=(e,),
        in_specs=[pl.BlockSpec((1, nb, LANE), lambda i: (i, 0, 0))],
        out_specs=[pl.BlockSpec((1, 1, cap), lambda i: (i, 0, 0)),
                   pl.BlockSpec((1, 1, cap), lambda i: (i, 0, 0)),
                   pl.BlockSpec((1, 1, nb), lambda i: (i, 0, 0))],
        out_shape=[jax.ShapeDtypeStruct((e, 1, cap), jnp.int32),
                   jax.ShapeDtypeStruct((e, 1, cap), F32),
                   jax.ShapeDtypeStruct((e, 1, nb), jnp.int32)],
        compiler_params=_params(("parallel",), 32),
        name="expert_select",
    )(aff3)


def _expert_up_kernel(idx_ref, h_hbm, wg_ref, wu_ref, o_ref, stage, xlo, xhi, sem, *, cap, phase_rows):
    e = pl.program_id(0)
    j = pl.program_id(1)
    half = xlo.shape[1]
    n_phases = cap // phase_rows
    unroll = SUBLANES
    assert phase_rows % unroll == 0

    def row_copy(p, r):
        tok = idx_ref[e * cap + p * phase_rows + r]
        return pltpu.make_async_copy(h_hbm.at[pl.ds(tok, 1), :],
                                     stage.at[p % 2, pl.ds(r, 1), :], sem.at[p % 2])

    def start_phase(p):
        def go(r, carry):
            for k in range(unroll):
                row_copy(p, r * unroll + k).start()
            return carry
        lax.fori_loop(0, phase_rows // unroll, go, 0)

    def wait_phase(p):
        def go(r, carry):
            for k in range(unroll):
                row_copy(p, r * unroll + k).wait()
            return carry
        lax.fori_loop(0, phase_rows // unroll, go, 0)

    @pl.when(j == 0)
    def _():
        start_phase(0)
        for p in range(n_phases):
            if p + 1 < n_phases:
                start_phase(p + 1)
            wait_phase(p)
            lo, hi = _unpack_bf16_pair(stage[p % 2])
            xlo[p * phase_rows:(p + 1) * phase_rows, :] = lo.astype(BF16)
            xhi[p * phase_rows:(p + 1) * phase_rows, :] = hi.astype(BF16)

    wg = wg_ref[0].astype(BF16)
    wu = wu_ref[0].astype(BF16)
    x_lo, x_hi = xlo[...], xhi[...]
    g = (jnp.dot(x_lo, wg[:half], preferred_element_type=F32)
         + jnp.dot(x_hi, wg[half:], preferred_element_type=F32))
    u = (jnp.dot(x_lo, wu[:half], preferred_element_type=F32)
         + jnp.dot(x_hi, wu[half:], preferred_element_type=F32))
    o_ref[0] = (g * jax.nn.sigmoid(g) * u).astype(o_ref.dtype)


def expert_up(idx_flat, h_packed, w_gate, w_up, cap):
    e, d, f = w_gate.shape
    half = d // 2
    tf = _tile(f, 256)
    phase_rows = _tile(cap, 512, 8)
    wspec = pl.BlockSpec((1, d, tf), lambda i, j, idx: (i, 0, j))
    kern = functools.partial(_expert_up_kernel, cap=cap, phase_rows=phase_rows)
    return pl.pallas_call(
        kern,
        grid_spec=pltpu.PrefetchScalarGridSpec(
            num_scalar_prefetch=1,
            grid=(e, f // tf),
            in_specs=[pl.BlockSpec(memory_space=pl.ANY), wspec, wspec],
            out_specs=pl.BlockSpec((1, cap, tf), lambda i, j, idx: (i, 0, j)),
            scratch_shapes=[pltpu.VMEM((2, phase_rows, half), jnp.int32),
                            pltpu.VMEM((cap, half), BF16),
                            pltpu.VMEM((cap, half), BF16),
                            pltpu.SemaphoreType.DMA((2,))],
        ),
        out_shape=jax.ShapeDtypeStruct((e, cap, f), BF16),
        compiler_params=_params(("arbitrary", "arbitrary"), 56),
        name="expert_up_gather",
    )(idx_flat, h_packed, w_gate, w_up)


def _expert_down_kernel(h_ref, wa_ref, wb_ref, v_ref, o_ref):
    h = h_ref[0]
    v = v_ref[0]
    ya = jnp.dot(h, wa_ref[0].astype(BF16), preferred_element_type=F32) * v
    yb = jnp.dot(h, wb_ref[0].astype(BF16), preferred_element_type=F32) * v
    o_ref[0] = _pack_bf16_pair(ya, yb)


def expert_down(hid, w_down, vals):
    e, c, f = hid.shape
    d = w_down.shape[2]
    half = d // 2
    tn = _tile(half, 256)
    nb = half // tn
    return pl.pallas_call(
        _expert_down_kernel,
        grid=(e, nb),
        in_specs=[pl.BlockSpec((1, c, f), lambda i, j: (i, 0, 0)),
                  pl.BlockSpec((1, f, tn), lambda i, j: (i, 0, j)),
                  pl.BlockSpec((1, f, tn), lambda i, j: (i, 0, nb + j)),
                  pl.BlockSpec((1, c, 1), lambda i, j: (i, 0, 0))],
        out_specs=pl.BlockSpec((1, c, tn), lambda i, j: (i, 0, j)),
        out_shape=jax.ShapeDtypeStruct((e, c, half), jnp.int32),
        compiler_params=_params(("parallel", "arbitrary"), 48),
        name="expert_down",
    )(hid, w_down, w_down, vals)


def _combine_kernel(idx_ref, soff_ref, x_ref, y_hbm, g2_ref, fg_ref, o_ref, stage, sem,
                    *, cap, n_blocks, n_experts):
    b = pl.program_id(0)
    tt, d = x_ref.shape
    half = d // 2
    cur = b % 2

    def slot_range(e, blk):
        lo = soff_ref[e * n_blocks + blk]
        nxt = soff_ref[e * n_blocks + jnp.minimum(blk + 1, n_blocks - 1)]
        return lo, jnp.where(blk + 1 < n_blocks, nxt, cap)

    def row_copy(e, j, blk, buf):
        tok = idx_ref[e * cap + j]
        return pltpu.make_async_copy(y_hbm.at[e, pl.ds(j, 1), :],
                                     stage.at[buf, e, pl.ds(tok - blk * tt, 1), :], sem.at[buf])

    def fetch(blk, buf):
        stage[buf] = jnp.zeros(stage.shape[1:], stage.dtype)
        for e in range(n_experts):
            lo, hi = slot_range(e, blk)

            def go(j, carry, e=e):
                row_copy(e, j, blk, buf).start()
                return carry
            lax.fori_loop(lo, hi, go, 0)

    def arrive(blk, buf):
        for e in range(n_experts):
            lo, hi = slot_range(e, blk)

            def done(j, carry, e=e):
                row_copy(e, j, blk, buf).wait()
                return carry
            lax.fori_loop(lo, hi, done, 0)

    @pl.when(b == 0)
    def _():
        fetch(b, cur)

    @pl.when(b + 1 < n_blocks)
    def _():
        fetch(b + 1, 1 - cur)

    arrive(b, cur)

    f_lo = jnp.zeros((tt, half), F32)
    f_hi = jnp.zeros((tt, half), F32)
    for e in range(n_experts):
        lo, hi = _unpack_bf16_pair(stage[cur, e])
        f_lo = f_lo + lo
        f_hi = f_hi + hi
    x = x_ref[...]
    x_lo = x[:, :half] + g2_ref[:, :half] * f_lo
    x_hi = x[:, half:] + g2_ref[:, half:] * f_hi
    ms = (jnp.sum(x_lo * x_lo, axis=-1, keepdims=True)
          + jnp.sum(x_hi * x_hi, axis=-1, keepdims=True)) * (1.0 / d)
    inv = lax.rsqrt(ms + EPS)
    o_ref[:, :half] = x_lo * inv * fg_ref[:, :half]
    o_ref[:, half:] = x_hi * inv * fg_ref[:, half:]


def combine_final_norm(idx_flat, soff_flat, x, y_packed, gate, final_g, cap):
    t, d = x.shape
    n_experts = y_packed.shape[0]
    tt = LANE
    n_blocks = t // tt
    vec = pl.BlockSpec((1, d), lambda i, a, b: (0, 0))
    row = pl.BlockSpec((tt, d), lambda i, a, b: (i, 0))
    kern = functools.partial(_combine_kernel, cap=cap, n_blocks=n_blocks, n_experts=n_experts)
    return pl.pallas_call(
        kern,
        grid_spec=pltpu.PrefetchScalarGridSpec(
            num_scalar_prefetch=2,
            grid=(n_blocks,),
            in_specs=[row, pl.BlockSpec(memory_space=pl.ANY), vec, vec],
            out_specs=row,
            scratch_shapes=[pltpu.VMEM((2, n_experts, tt, d // 2), jnp.int32),
                            pltpu.SemaphoreType.DMA((2,))],
        ),
        out_shape=jax.ShapeDtypeStruct((t, d), F32),
        compiler_params=_params(("arbitrary",), 56),
        name="combine_final_norm",
    )(idx_flat, soff_flat, x, y_packed, gate, final_g)


def kernel(x, c, ctx, c_ctx, w_ada, b_ada, norm1_g, w_in, b_in, q_norm_g, k_norm_g, w_attn_o,
           dw_w, dw_b, conv_ln_g, conv_ln_b, w_conv_o, w_out, norm2_g, w_router, w_gate, w_up,
           w_down, final_g):
    batch, t, d = x.shape
    depth = w_ada.shape[0]
    assert batch == 1 and depth == 1
    q_w = w_attn_o.shape[1]
    conv_w = w_conv_o.shape[1]
    in_w = w_in.shape[2]
    kv_w = (in_w - q_w - 2 * conv_w - 2 * d) // 2
    n_kv = kv_w // HEAD_DIM
    n_experts = w_router.shape[2]
    cap = CAPACITY_FACTOR * t // n_experts
    o1, o2, o3, o4 = q_w, q_w + kv_w, q_w + 2 * kv_w, q_w + 2 * kv_w + 2 * conv_w

    row = lambda v: v.reshape(1, -1)
    x2, ctx2 = x[0], ctx[0]

    cvecs_t = jnp.stack([c[0], c_ctx], axis=1)
    mod = ada_modulation(cvecs_t, w_ada[0], row(b_ada[0]))
    sh1, sc1, g1, sh2, sc2, g2 = [mod[0:1, i * d:(i + 1) * d] for i in range(N_MOD)]
    csh1, csc1 = mod[1:2, 0:d], mod[1:2, d:2 * d]

    h_lat = norm_modulate(x2, row(norm1_g[0]), sc1, sh1)
    h_ctx = norm_modulate(ctx2, row(norm1_g[0]), csc1, csh1)

    w_in0, b_in0 = w_in[0], row(b_in[0])
    w_qkv = w_in0[:, :o3].astype(BF16)
    cosf, sins = rope_tables(t)
    qg, kg = row(q_norm_g[0]), row(k_norm_g[0])
    gw = q_w // n_kv
    assert gw == kv_w
    q_scale = HEAD_DIM ** -0.5 * LOG2E
    qkv = qkv_projection(h_lat, w_qkv, b_in0[:, :o3], qg, kg, cosf, sins,
                         block_w=gw, n_q_blocks=n_kv, q_scale=q_scale)
    u = matmul_bias(h_lat, w_in0[:, o3:o4].astype(BF16), b_in0[:, o3:o4])
    gates = matmul_bias(h_lat, w_in0[:, o4:].astype(BF16), b_in0[:, o4:])
    kv_ctx = matmul_bias(h_ctx, w_qkv[:, o1:o3], b_in0[:, o1:o3])

    tc = ctx2.shape[0]
    k_ctx = head_norm(kv_ctx, 0, 1, kv_w, kg, cosf[:tc], sins[:tc], rope=False, scale=1.0)
    k_all = jnp.concatenate([k_ctx, qkv[:, o1:o2]], axis=0)
    v_all = jnp.concatenate([kv_ctx[:, kv_w:], qkv[:, o2:o3]], axis=0)
    score_bound = (1.02 * HEAD_DIM * q_scale) * jnp.max(jnp.abs(q_norm_g[0])) * jnp.max(jnp.abs(k_norm_g[0]))
    attn = gqa_attention(qkv, q_w, k_all, v_all, n_kv, score_bound)

    conv = conformer_conv_pre(u, dw_w[0], row(dw_b[0]), row(conv_ln_g[0]), row(conv_ln_b[0]))
    merged = gated_merge(attn, w_attn_o[0].astype(BF16), conv, w_conv_o[0].astype(BF16), gates)
    x1 = matmul_gated_residual(merged, w_out[0].astype(BF16), x2, g1)

    h2p, aff = norm_router(x1, row(norm2_g[0]), sc2, sh2, w_router[0])
    aff3 = aff[:, :n_experts].T.reshape(n_experts, t // LANE, LANE)
    idx, vals, soff = expert_select(aff3, cap)
    idx_flat = idx.reshape(-1)
    hid = expert_up(idx_flat, h2p, w_gate[0], w_up[0], cap)
    y = expert_down(hid, w_down[0], vals.reshape(n_experts, cap, 1))
    out = combine_final_norm(idx_flat, soff.reshape(-1), x1, y, g2, row(final_g), cap)
    return out[None]
```

```python
import functools

import jax
import jax.numpy as jnp
from jax import lax
from jax.experimental import pallas as pl
from jax.experimental.pallas import tpu as pltpu

F32 = jnp.float32
BF16 = jnp.bfloat16

HEAD_DIM = 128
GRID_W = 64
ROPE_THETA = 10000.0
ROPE_AXIS_DIM = HEAD_DIM // 2
EPS = 1e-6
CAPACITY_FACTOR = 2
N_MOD = 6
HALO = 16
LANE = 128
SUBLANES = 8
MIB = 1024 * 1024
LOG2E = 1.4426950408889634


def _tile(n, pref, mult=LANE):
    if n <= pref:
        return n
    t = (pref // mult) * mult
    while t >= mult:
        if n % t == 0:
            return t
        t -= mult
    return n


def _params(sem, vmem_mib):
    return pltpu.CompilerParams(dimension_semantics=sem, vmem_limit_bytes=vmem_mib * MIB)


def _ada_kernel(s_ref, w_ref, b_ref, o_ref):
    k = pl.program_id(1)

    @pl.when(k == 0)
    def _():
        o_ref[...] = jnp.broadcast_to(b_ref[...], o_ref.shape)

    s = s_ref[...]
    s = s * jax.nn.sigmoid(s)
    w = w_ref[...]
    r0 = jnp.sum(s[:, 0:1] * w, axis=0, keepdims=True)
    r1 = jnp.sum(s[:, 1:2] * w, axis=0, keepdims=True)
    o_ref[0:1, :] += r0
    o_ref[1:2, :] += r1


def ada_modulation(cvecs_t, w_ada, b_ada):
    d, n = w_ada.shape
    tk = _tile(d, 1024, 8)
    tn = _tile(n, 2048)
    return pl.pallas_call(
        _ada_kernel,
        grid=(n // tn, d // tk),
        in_specs=[
            pl.BlockSpec((tk, 2), lambda j, k: (k, 0)),
            pl.BlockSpec((tk, tn), lambda j, k: (k, j)),
            pl.BlockSpec((1, tn), lambda j, k: (0, j)),
        ],
        out_specs=pl.BlockSpec((2, tn), lambda j, k: (0, j)),
        out_shape=jax.ShapeDtypeStruct((2, n), F32),
        compiler_params=_params(("parallel", "arbitrary"), 48),
        name="ada_modulation",
    )(cvecs_t, w_ada, b_ada)


def _normmod_kernel(x_ref, g_ref, sc_ref, sh_ref, o_ref):
    x = x_ref[...].astype(F32)
    ms = jnp.mean(x * x, axis=-1, keepdims=True)
    y = x * lax.rsqrt(ms + EPS) * g_ref[...]
    o_ref[...] = (y * (1.0 + sc_ref[...]) + sh_ref[...]).astype(o_ref.dtype)


def norm_modulate(x, g, scale, shift, out_dtype=BF16):
    t, d = x.shape
    tm = _tile(t, 256, 8)
    vec = pl.BlockSpec((1, d), lambda i: (0, 0))
    return pl.pallas_call(
        _normmod_kernel,
        grid=(t // tm,),
        in_specs=[pl.BlockSpec((tm, d), lambda i: (i, 0)), vec, vec, vec],
        out_specs=pl.BlockSpec((tm, d), lambda i: (i, 0)),
        out_shape=jax.ShapeDtypeStruct((t, d), out_dtype),
        compiler_params=_params(("parallel",), 40),
        name="norm_modulate",
    )(x, g, scale, shift)


def _mm_bias_kernel(a_ref, w_ref, b_ref, o_ref):
    acc = jnp.dot(a_ref[...], w_ref[...], preferred_element_type=F32)
    o_ref[...] = (acc + b_ref[...]).astype(o_ref.dtype)


def matmul_bias(a, w, b, out_dtype=BF16, tm_pref=1024, tn_pref=1024):
    m, k = a.shape
    n = w.shape[1]
    tm = _tile(m, tm_pref, 8)
    tn = _tile(n, tn_pref)
    return pl.pallas_call(
        _mm_bias_kernel,
        grid=(n // tn, m // tm),
        in_specs=[
            pl.BlockSpec((tm, k), lambda j, i: (i, 0)),
            pl.BlockSpec((k, tn), lambda j, i: (0, j)),
            pl.BlockSpec((1, tn), lambda j, i: (0, j)),
        ],
        out_specs=pl.BlockSpec((tm, tn), lambda j, i: (i, j)),
        out_shape=jax.ShapeDtypeStruct((m, n), out_dtype),
        compiler_params=_params(("parallel", "parallel"), 56),
        name="matmul_bias",
    )(a, w, b)


def _norm_heads(x, g, o_ref, *, scale, cosf=None, sins=None):
    if cosf is not None:
        lane = lax.broadcasted_iota(jnp.int32, cosf.shape, 1)
        first_half = (lane % (ROPE_AXIS_DIM)) < (ROPE_AXIS_DIM // 2)
    for h in range(x.shape[1] // HEAD_DIM):
        xh = x[:, h * HEAD_DIM:(h + 1) * HEAD_DIM]
        ms = jnp.mean(xh * xh, axis=-1, keepdims=True)
        y = xh * lax.rsqrt(ms + EPS) * g
        if cosf is not None:
            up = pltpu.roll(y, HEAD_DIM - ROPE_AXIS_DIM // 2, axis=1)
            dn = pltpu.roll(y, ROPE_AXIS_DIM // 2, axis=1)
            y = y * cosf + jnp.where(first_half, up, dn) * sins
        if scale != 1.0:
            y = y * scale
        o_ref[:, h * HEAD_DIM:(h + 1) * HEAD_DIM] = y.astype(o_ref.dtype)


def _headnorm_kernel(x_ref, g_ref, cos_ref, sin_ref, o_ref, *, n_heads, rope, scale):
    x = x_ref[...].astype(F32)
    if rope:
        _norm_heads(x, g_ref[...], o_ref, scale=scale, cosf=cos_ref[...], sins=sin_ref[...])
    else:
        _norm_heads(x, g_ref[...], o_ref, scale=scale)


def _qkv_kernel(a_ref, w_ref, b_ref, qg_ref, kg_ref, cos_ref, sin_ref, o_ref, *, n_q_blocks, q_scale):
    j = pl.program_id(0)
    acc = jnp.dot(a_ref[...], w_ref[...], preferred_element_type=F32) + b_ref[...]

    @pl.when(j < n_q_blocks)
    def _():
        _norm_heads(acc, qg_ref[...], o_ref, scale=q_scale, cosf=cos_ref[...], sins=sin_ref[...])

    @pl.when(j == n_q_blocks)
    def _():
        _norm_heads(acc, kg_ref[...], o_ref, scale=1.0, cosf=cos_ref[...], sins=sin_ref[...])

    @pl.when(j > n_q_blocks)
    def _():
        o_ref[...] = acc.astype(o_ref.dtype)


def qkv_projection(a, w, b, qg, kg, cos, sin, *, block_w, n_q_blocks, q_scale):
    m, k = a.shape
    n = w.shape[1]
    assert n == (n_q_blocks + 2) * block_w
    tm = _tile(m, 1024, 8)
    kern = functools.partial(_qkv_kernel, n_q_blocks=n_q_blocks, q_scale=q_scale)
    gain = pl.BlockSpec((1, HEAD_DIM), lambda j, i: (0, 0))
    tab = pl.BlockSpec((tm, HEAD_DIM), lambda j, i: (i, 0))
    return pl.pallas_call(
        kern,
        grid=(n // block_w, m // tm),
        in_specs=[
            pl.BlockSpec((tm, k), lambda j, i: (i, 0)),
            pl.BlockSpec((k, block_w), lambda j, i: (0, j)),
            pl.BlockSpec((1, block_w), lambda j, i: (0, j)),
            gain, gain, tab, tab,
        ],
        out_specs=pl.BlockSpec((tm, block_w), lambda j, i: (i, j)),
        out_shape=jax.ShapeDtypeStruct((m, n), BF16),
        compiler_params=_params(("parallel", "parallel"), 56),
        name="qkv_projection",
    )(a, w, b, qg, kg, cos, sin)


def head_norm(x, col_block, n_blocks, width, g, cos, sin, *, rope, scale):
    t = x.shape[0]
    tm = _tile(t, 512, 8)
    n_heads = width // HEAD_DIM
    kern = functools.partial(_headnorm_kernel, n_heads=n_heads, rope=rope, scale=scale)
    tab = pl.BlockSpec((tm, HEAD_DIM), lambda i, j: (i, 0))
    return pl.pallas_call(
        kern,
        grid=(t // tm, n_blocks),
        in_specs=[
            pl.BlockSpec((tm, width), lambda i, j: (i, col_block + j)),
            pl.BlockSpec((1, HEAD_DIM), lambda i, j: (0, 0)),
            tab, tab,
        ],
        out_specs=pl.BlockSpec((tm, width), lambda i, j: (i, j)),
        out_shape=jax.ShapeDtypeStruct((t, n_blocks * width), BF16),
        compiler_params=_params(("parallel", "parallel"), 32),
        name="head_norm_rope",
    )(x, g, cos, sin)


def rope_tables(n_tokens):
    n_rows = n_tokens // GRID_W
    row = jnp.repeat(jnp.arange(n_rows, dtype=jnp.int32), GRID_W)
    col = jnp.tile(jnp.arange(GRID_W, dtype=jnp.int32), n_rows)
    pos = jnp.stack([row, col], axis=-1).astype(F32)
    inv_freq = ROPE_THETA ** (-jnp.arange(0, ROPE_AXIS_DIM, 2, dtype=F32) / ROPE_AXIS_DIM)
    ang = pos[:, :, None] * inv_freq
    cos, sin = jnp.cos(ang), jnp.sin(ang)
    cosf = jnp.concatenate([cos, cos], axis=-1).reshape(n_tokens, HEAD_DIM)
    sins = jnp.concatenate([-sin, sin], axis=-1).reshape(n_tokens, HEAD_DIM)
    return cosf, sins


def _attn_kernel(b_ref, q_ref, k_ref, vt_ref, o_ref, acc_sc, *, group, n_chunks, bounded):
    tq = q_ref.shape[0]
    acc_sc[...] = jnp.zeros_like(acc_sc)
    qs = [q_ref[:, g * HEAD_DIM:(g + 1) * HEAD_DIM] for g in range(group)]
    shift = b_ref[0, 0]

    def body(c, carry):
        ms, ls = carry
        kc = k_ref[0, c]
        vc = vt_ref[0, c]
        new_m, new_l = [], []
        for g in range(group):
            s = lax.dot_general(kc, qs[g], (((1,), (1,)), ((), ())),
                                preferred_element_type=F32)
            if bounded:
                p = jnp.exp2(s - shift)
                new_m.append(ms[g])
                new_l.append(ls[g] + jnp.sum(p, axis=0, keepdims=True))
                acc_sc[g] += jnp.dot(vc, p.astype(BF16), preferred_element_type=F32)
            else:
                m_new = jnp.maximum(ms[g], jnp.max(s, axis=0, keepdims=True))
                p = jnp.exp2(s - m_new)
                alpha = jnp.exp2(ms[g] - m_new)
                new_m.append(m_new)
                new_l.append(alpha * ls[g] + jnp.sum(p, axis=0, keepdims=True))
                pv = jnp.dot(vc, p.astype(BF16), preferred_element_type=F32)
                acc_sc[g] = alpha * acc_sc[g] + pv
        return tuple(new_m), tuple(new_l)

    m0 = tuple(jnp.full((1, tq), -jnp.inf, F32) for _ in range(group))
    l0 = tuple(jnp.zeros((1, tq), F32) for _ in range(group))
    _, ls = lax.fori_loop(0, n_chunks, body, (m0, l0))
    for g in range(group):
        out_t = acc_sc[g] / ls[g]
        o_ref[:, g * HEAD_DIM:(g + 1) * HEAD_DIM] = out_t.T.astype(o_ref.dtype)


MAX_SAFE_SCORE_BOUND = 60.0


def gqa_attention(q, qw, k_all, v_all, n_kv_heads, score_bound):
    t = q.shape[0]
    tk_all = k_all.shape[0]
    group = qw // HEAD_DIM // n_kv_heads
    gw = group * HEAD_DIM
    tq = _tile(t, 512)
    tkc = _tile(tk_all, 1280)
    nch = tk_all // tkc
    k4 = k_all.reshape(nch, tkc, n_kv_heads, HEAD_DIM).transpose(2, 0, 1, 3)
    vt4 = v_all.reshape(nch, tkc, n_kv_heads, HEAD_DIM).transpose(2, 0, 3, 1)
    shift = score_bound.reshape(1, 1).astype(F32)

    def call(bounded):
        kern = functools.partial(_attn_kernel, group=group, n_chunks=nch, bounded=bounded)
        return pl.pallas_call(
            kern,
            grid=(n_kv_heads, t // tq),
            in_specs=[
                pl.BlockSpec(memory_space=pltpu.SMEM),
                pl.BlockSpec((tq, gw), lambda h, i: (i, h)),
                pl.BlockSpec((1, nch, tkc, HEAD_DIM), lambda h, i: (h, 0, 0, 0)),
                pl.BlockSpec((1, nch, HEAD_DIM, tkc), lambda h, i: (h, 0, 0, 0)),
            ],
            out_specs=pl.BlockSpec((tq, gw), lambda h, i: (i, h)),
            out_shape=jax.ShapeDtypeStruct((t, qw), BF16),
            scratch_shapes=[pltpu.VMEM((group, HEAD_DIM, tq), F32)],
            compiler_params=_params(("parallel", "parallel"), 48),
            name="gqa_flash_attention_bounded" if bounded else "gqa_flash_attention_online",
        )(shift, q, k4, vt4)

    return lax.cond(score_bound < MAX_SAFE_SCORE_BOUND, lambda: call(True), lambda: call(False))


def _conv_kernel(a_ref, g_ref, ap_ref, gp_ref, an_ref, gn_ref, w_ref, b_ref, lg_ref, lb_ref,
                 o_ref, y_sc, c_sc, sh_sc, *, ksize, rows, lanes):
    i = pl.program_id(0)
    n = pl.num_programs(0)
    tt, cw = o_ref.shape

    def glu(a, g):
        a = a.astype(F32)
        g = g.astype(F32)
        return a * jax.nn.sigmoid(g)

    y_sc[HALO:HALO + tt, :] = glu(a_ref[...], g_ref[...])
    y_sc[0:HALO, :] = jnp.where(i > 0, glu(ap_ref[...], gp_ref[...]), 0.0)
    y_sc[HALO + tt:HALO + tt + HALO, :] = jnp.where(i < n - 1, glu(an_ref[...], gn_ref[...]), 0.0)

    base = HALO - ksize // 2
    span = sh_sc.shape[1]
    for c0 in range(0, cw, lanes):
        wc = w_ref[:, c0:c0 + lanes]
        bc = b_ref[:, c0:c0 + lanes]
        for s in range(SUBLANES):
            sh_sc[s] = y_sc[s:s + span, c0:c0 + lanes]
        for r0 in range(0, tt, rows):
            acc = jnp.broadcast_to(bc, (rows, lanes))
            for j in range(ksize):
                s = (base + j) % SUBLANES
                top = r0 + base + j - s
                acc = acc + wc[j:j + 1, :] * sh_sc[s, top:top + rows, :]
            c_sc[r0:r0 + rows, c0:c0 + lanes] = acc

    y = c_sc[...]
    mu = jnp.mean(y, axis=-1, keepdims=True)
    yc = y - mu
    var = jnp.mean(yc * yc, axis=-1, keepdims=True)
    z = yc * lax.rsqrt(var + EPS) * lg_ref[...] + lb_ref[...]
    o_ref[...] = (z * jax.nn.sigmoid(z)).astype(o_ref.dtype)


def conformer_conv_pre(u, dw_w, dw_b, ln_g, ln_b):
    t, c2 = u.shape
    cw = c2 // 2
    ksize = dw_w.shape[0]
    assert ksize // 2 <= HALO - 1
    tt = _tile(t, 256, HALO)
    hb = tt // HALO
    n_hb = t // HALO
    cur_a = pl.BlockSpec((tt, cw), lambda i: (i, 0))
    cur_g = pl.BlockSpec((tt, cw), lambda i: (i, 1))
    prev_a = pl.BlockSpec((HALO, cw), lambda i: (jnp.maximum(i * hb - 1, 0), 0))
    prev_g = pl.BlockSpec((HALO, cw), lambda i: (jnp.maximum(i * hb - 1, 0), 1))
    next_a = pl.BlockSpec((HALO, cw), lambda i: (jnp.minimum((i + 1) * hb, n_hb - 1), 0))
    next_g = pl.BlockSpec((HALO, cw), lambda i: (jnp.minimum((i + 1) * hb, n_hb - 1), 1))
    vec = pl.BlockSpec((1, cw), lambda i: (0, 0))
    lanes = min(512, cw)
    kern = functools.partial(_conv_kernel, ksize=ksize, rows=min(64, tt), lanes=lanes)
    return pl.pallas_call(
        kern,
        grid=(t // tt,),
        in_specs=[cur_a, cur_g, prev_a, prev_g, next_a, next_g,
                  pl.BlockSpec((ksize, cw), lambda i: (0, 0)), vec, vec, vec],
        out_specs=pl.BlockSpec((tt, cw), lambda i: (i, 0)),
        out_shape=jax.ShapeDtypeStruct((t, cw), BF16),
        scratch_shapes=[pltpu.VMEM((tt + 2 * HALO, cw), F32), pltpu.VMEM((tt, cw), F32),
                        pltpu.VMEM((SUBLANES, tt + 2 * HALO - SUBLANES, lanes), F32)],
        compiler_params=_params(("parallel",), 40),
        name="conformer_conv",
    )(u, u, u, u, u, u, dw_w, dw_b, ln_g, ln_b)


def _merge_kernel(a_ref, wa_ref, b_ref, wb_ref, ga_ref, gb_ref, o_ref):
    pa = jnp.dot(a_ref[...], wa_ref[...], preferred_element_type=F32)
    pb = jnp.dot(b_ref[...], wb_ref[...], preferred_element_type=F32)
    ga = jax.nn.sigmoid(ga_ref[...].astype(F32))
    gb = jax.nn.sigmoid(gb_ref[...].astype(F32))
    o_ref[...] = (ga * pa + gb * pb).astype(o_ref.dtype)


def gated_merge(attn, w_attn_o, conv, w_conv_o, gates):
    t, ka = attn.shape
    kb = conv.shape[1]
    d = w_attn_o.shape[1]
    tm = _tile(t, 512, 8)
    tn = _tile(d, 1024)
    nb = d // tn
    return pl.pallas_call(
        _merge_kernel,
        grid=(nb, t // tm),
        in_specs=[
            pl.BlockSpec((tm, ka), lambda j, i: (i, 0)),
            pl.BlockSpec((ka, tn), lambda j, i: (0, j)),
            pl.BlockSpec((tm, kb), lambda j, i: (i, 0)),
            pl.BlockSpec((kb, tn), lambda j, i: (0, j)),
            pl.BlockSpec((tm, tn), lambda j, i: (i, j)),
            pl.BlockSpec((tm, tn), lambda j, i: (i, nb + j)),
        ],
        out_specs=pl.BlockSpec((tm, tn), lambda j, i: (i, j)),
        out_shape=jax.ShapeDtypeStruct((t, d), BF16),
        compiler_params=_params(("parallel", "parallel"), 48),
        name="gated_merge",
    )(attn, w_attn_o, conv, w_conv_o, gates, gates)


def _mm_resid_kernel(a_ref, w_ref, x_ref, g_ref, o_ref):
    acc = jnp.dot(a_ref[...], w_ref[...], preferred_element_type=F32)
    o_ref[...] = x_ref[...] + g_ref[...] * acc


def matmul_gated_residual(a, w, x, gate):
    m, k = a.shape
    n = w.shape[1]
    tm = _tile(m, 1024, 8)
    tn = _tile(n, 1024)
    return pl.pallas_call(
        _mm_resid_kernel,
        grid=(n // tn, m // tm),
        in_specs=[
            pl.BlockSpec((tm, k), lambda j, i: (i, 0)),
            pl.BlockSpec((k, tn), lambda j, i: (0, j)),
            pl.BlockSpec((tm, tn), lambda j, i: (i, j)),
            pl.BlockSpec((1, tn), lambda j, i: (0, j)),
        ],
        out_specs=pl.BlockSpec((tm, tn), lambda j, i: (i, j)),
        out_shape=jax.ShapeDtypeStruct((m, n), F32),
        compiler_params=_params(("parallel", "parallel"), 56),
        name="out_proj_residual",
    )(a, w, x, gate)


HI_HALF_MASK = -65536


def _pack_bf16_pair(lo, hi):
    lo_b = lax.bitcast_convert_type(lo.astype(BF16).astype(F32), jnp.int32)
    hi_b = lax.bitcast_convert_type(hi.astype(BF16).astype(F32), jnp.int32)
    return (hi_b & HI_HALF_MASK) | lax.shift_right_logical(lo_b, jnp.full(lo_b.shape, 16, jnp.int32))


def _unpack_bf16_pair(w):
    lo = lax.bitcast_convert_type(w << 16, F32)
    hi = lax.bitcast_convert_type(w & HI_HALF_MASK, F32)
    return lo, hi


def _router_kernel(x_ref, g_ref, sc_ref, sh_ref, wh_ref, wl_ref, h_ref, aff_ref, *, n_experts):
    x = x_ref[...]
    half = x.shape[1] // 2
    ms = jnp.mean(x * x, axis=-1, keepdims=True)
    h = x * lax.rsqrt(ms + EPS) * g_ref[...]
    h = h * (1.0 + sc_ref[...]) + sh_ref[...]
    hb = h.astype(BF16)
    h_ref[...] = _pack_bf16_pair(h[:, :half], h[:, half:])
    hl = (h - hb.astype(F32)).astype(BF16)
    logits = (jnp.dot(hb, wh_ref[...], preferred_element_type=F32)
              + jnp.dot(hb, wl_ref[...], preferred_element_type=F32)
              + jnp.dot(hl, wh_ref[...], preferred_element_type=F32))
    lane = lax.broadcasted_iota(jnp.int32, logits.shape, 1)
    logits = jnp.where(lane < n_experts, logits, -jnp.inf)
    mx = jnp.max(logits, axis=-1, keepdims=True)
    e = jnp.exp(logits - mx)
    aff_ref[...] = e / jnp.sum(e, axis=-1, keepdims=True)


def norm_router(x, g, scale, shift, w_router):
    t, d = x.shape
    n_experts = w_router.shape[1]
    wp = jnp.zeros((d, LANE), F32).at[:, :n_experts].set(w_router)
    wh = wp.astype(BF16)
    wl = (wp - wh.astype(F32)).astype(BF16)
    tm = _tile(t, 256, 8)
    vec = pl.BlockSpec((1, d), lambda i: (0, 0))
    wspec = pl.BlockSpec((d, LANE), lambda i: (0, 0))
    kern = functools.partial(_router_kernel, n_experts=n_experts)
    return pl.pallas_call(
        kern,
        grid=(t // tm,),
        in_specs=[pl.BlockSpec((tm, d), lambda i: (i, 0)), vec, vec, vec, wspec, wspec],
        out_specs=[pl.BlockSpec((tm, d // 2), lambda i: (i, 0)), pl.BlockSpec((tm, LANE), lambda i: (i, 0))],
        out_shape=[jax.ShapeDtypeStruct((t, d // 2), jnp.int32), jax.ShapeDtypeStruct((t, LANE), F32)],
        compiler_params=_params(("parallel",), 40),
        name="norm_router",
    )(x, g, scale, shift, wh, wl)


def _block_cumsum(m):
    nb = m.shape[0]
    r = lax.broadcasted_iota(jnp.int32, (LANE, LANE), 0)
    c = lax.broadcasted_iota(jnp.int32, (LANE, LANE), 1)
    upper = jnp.where(r <= c, 1.0, 0.0).astype(BF16)
    inblock = jnp.dot(m.astype(BF16), upper, preferred_element_type=F32)
    tot = jnp.broadcast_to(inblock[:, LANE - 1:LANE], (nb, LANE))
    rb = lax.broadcasted_iota(jnp.int32, (nb, nb), 0)
    cb = lax.broadcasted_iota(jnp.int32, (nb, nb), 1)
    lower = jnp.where(cb < rb, 1.0, 0.0).astype(BF16)
    before = jnp.dot(lower, tot.astype(BF16), preferred_element_type=F32)
    return inblock, before


def _select_kernel(a_ref, idx_ref, val_ref, soff_ref, *, cap):
    a = a_ref[0]
    nb = a.shape[0]
    bits = lax.bitcast_convert_type(a, jnp.int32)
    capf = jnp.float32(cap)

    def count(pred):
        cnt = jnp.sum(jnp.where(pred, 1.0, 0.0), axis=0, keepdims=True)
        return jnp.sum(cnt, axis=1, keepdims=True)

    def bisect(i, thr):
        cand = thr | lax.shift_left(jnp.int32(1), 30 - i)
        return jnp.where(count(bits >= cand) >= capf, cand, thr)

    thr = lax.fori_loop(0, 31, bisect, jnp.zeros((1, 1), jnp.int32))
    gt = bits > thr
    eq = bits == thr
    need = capf - count(gt)
    eqf = jnp.where(eq, 1.0, 0.0)
    eq_in, eq_before = _block_cumsum(eqf)
    take_eq = eq & ((eq_in + eq_before - eqf) < need)
    sel = jnp.where(gt | take_eq, 1.0, 0.0)
    inblock, before = _block_cumsum(sel)

    slot = lax.broadcasted_iota(jnp.int32, (1, cap), 1).astype(F32)
    upto = before[:, 0:1] + inblock[:, LANE - 1:LANE]
    blk = jnp.sum(jnp.where(upto <= slot, 1.0, 0.0), axis=0, keepdims=True)
    bi = lax.broadcasted_iota(jnp.int32, (nb, cap), 0).astype(F32)
    onehot = jnp.where(bi == blk, 1.0, 0.0).astype(BF16)

    rows = jnp.dot(inblock.T.astype(BF16), onehot, preferred_element_type=F32)
    before_t = before.T
    hi = jnp.floor(before_t * (1.0 / LANE))
    lo = before_t - hi * LANE
    start = (jnp.dot(hi.astype(BF16), onehot, preferred_element_type=F32) * LANE
             + jnp.dot(lo.astype(BF16), onehot, preferred_element_type=F32))
    rank = slot - start
    lane_of = jnp.sum(jnp.where(rows <= rank, 1.0, 0.0), axis=0, keepdims=True)
    idx_ref[0] = (blk * LANE + lane_of).astype(jnp.int32)

    at = a.T
    a1 = at.astype(BF16)
    r1 = at - a1.astype(F32)
    a2 = r1.astype(BF16)
    a3 = (r1 - a2.astype(F32)).astype(BF16)
    picked = (jnp.dot(a1, onehot, preferred_element_type=F32)
              + jnp.dot(a2, onehot, preferred_element_type=F32)
              + jnp.dot(a3, onehot, preferred_element_type=F32))
    li = lax.broadcasted_iota(jnp.int32, (LANE, cap), 0).astype(F32)
    val_ref[0] = jnp.sum(jnp.where(li == lane_of, picked, 0.0), axis=0, keepdims=True)
    soff_ref[0] = before_t[0:1, :].astype(jnp.int32)


def expert_select(aff3, cap):
    e, nb, _ = aff3.shape
    kern = functools.partial(_select_kernel, cap=cap)
    return pl.pallas_call(
        kern,
        grid=(e,),
        in_specs=[pl.BlockSpec((1, nb, LANE), lambda i: (i, 0, 0))],
        out_specs=[pl.BlockSpec((1, 1, cap), lambda i: (i, 0, 0)),
                   pl.BlockSpec((1, 1, cap), lambda i: (i, 0, 0)),
                   pl.BlockSpec((1, 1, nb), lambda i: (i, 0, 0))],
        out_shape=[jax.ShapeDtypeStruct((e, 1, cap), jnp.int32),
                   jax.ShapeDtypeStruct((e, 1, cap), F32),
                   jax.ShapeDtypeStruct((e, 1, nb), jnp.int32)],
        compiler_params=_params(("parallel",), 32),
        name="expert_select",
    )(aff3)


def _expert_up_kernel(idx_ref, h_hbm, wg_ref, wu_ref, o_ref, stage, x_sc, sem, *, cap, phase_rows):
    e = pl.program_id(0)
    j = pl.program_id(1)
    half = x_sc.shape[1] // 2
    n_phases = cap // phase_rows
    unroll = SUBLANES
    assert phase_rows % unroll == 0

    def row_copy(p, r):
        tok = idx_ref[e * cap + p * phase_rows + r]
        return pltpu.make_async_copy(h_hbm.at[pl.ds(tok, 1), :],
                                     stage.at[p % 2, pl.ds(r, 1), :], sem.at[p % 2])

    def start_phase(p):
        def go(r, carry):
            for k in range(unroll):
                row_copy(p, r * unroll + k).start()
            return carry
        lax.fori_loop(0, phase_rows // unroll, go, 0)

    def wait_phase(p):
        def go(r, carry):
            for k in range(unroll):
                row_copy(p, r * unroll + k).wait()
            return carry
        lax.fori_loop(0, phase_rows // unroll, go, 0)

    @pl.when(j == 0)
    def _():
        start_phase(0)
        for p in range(n_phases):
            if p + 1 < n_phases:
                start_phase(p + 1)
            wait_phase(p)
            lo, hi = _unpack_bf16_pair(stage[p % 2])
            x_sc[p * phase_rows:(p + 1) * phase_rows, :half] = lo.astype(BF16)
            x_sc[p * phase_rows:(p + 1) * phase_rows, half:] = hi.astype(BF16)

    x = x_sc[...]
    g = jnp.dot(x, wg_ref[0].astype(BF16), preferred_element_type=F32)
    u = jnp.dot(x, wu_ref[0].astype(BF16), preferred_element_type=F32)
    o_ref[0] = (g * jax.nn.sigmoid(g) * u).astype(o_ref.dtype)


def expert_up(idx_flat, h_packed, w_gate, w_up, cap):
    e, d, f = w_gate.shape
    half = d // 2
    tf = _tile(f, 256)
    phase_rows = _tile(cap, 512, 8)
    wspec = pl.BlockSpec((1, d, tf), lambda i, j, idx: (i, 0, j))
    kern = functools.partial(_expert_up_kernel, cap=cap, phase_rows=phase_rows)
    return pl.pallas_call(
        kern,
        grid_spec=pltpu.PrefetchScalarGridSpec(
            num_scalar_prefetch=1,
            grid=(e, f // tf),
            in_specs=[pl.BlockSpec(memory_space=pl.ANY), wspec, wspec],
            out_specs=pl.BlockSpec((1, cap, tf), lambda i, j, idx: (i, 0, j)),
            scratch_shapes=[pltpu.VMEM((2, phase_rows, half), jnp.int32),
                            pltpu.VMEM((cap, d), BF16),
                            pltpu.SemaphoreType.DMA((2,))],
        ),
        out_shape=jax.ShapeDtypeStruct((e, cap, f), BF16),
        compiler_params=_params(("arbitrary", "arbitrary"), 56),
        name="expert_up_gather",
    )(idx_flat, h_packed, w_gate, w_up)


def _expert_down_kernel(h_ref, wa_ref, wb_ref, v_ref, o_ref):
    h = h_ref[0]
    v = v_ref[0]
    ya = jnp.dot(h, wa_ref[0].astype(BF16), preferred_element_type=F32) * v
    yb = jnp.dot(h, wb_ref[0].astype(BF16), preferred_element_type=F32) * v
    o_ref[0] = _pack_bf16_pair(ya, yb)


def expert_down(hid, w_down, vals):
    e, c, f = hid.shape
    d = w_down.shape[2]
    half = d // 2
    tn = _tile(half, 256)
    nb = half // tn
    return pl.pallas_call(
        _expert_down_kernel,
        grid=(e, nb),
        in_specs=[pl.BlockSpec((1, c, f), lambda i, j: (i, 0, 0)),
                  pl.BlockSpec((1, f, tn), lambda i, j: (i, 0, j)),
                  pl.BlockSpec((1, f, tn), lambda i, j: (i, 0, nb + j)),
                  pl.BlockSpec((1, c, 1), lambda i, j: (i, 0, 0))],
        out_specs=pl.BlockSpec((1, c, tn), lambda i, j: (i, 0, j)),
        out_shape=jax.ShapeDtypeStruct((e, c, half), jnp.int32),
        compiler_params=_params(("parallel", "arbitrary"), 48),
        name="expert_down",
    )(hid, w_down, w_down, vals)


def _combine_kernel(idx_ref, soff_ref, x_ref, y_hbm, g2_ref, fg_ref, o_ref, stage, sem,
                    *, cap, n_blocks, n_experts):
    b = pl.program_id(0)
    tt, d = x_ref.shape
    half = d // 2
    n_groups = sem.shape[0]
    per_group = n_experts // n_groups

    def slot_range(e):
        lo = soff_ref[e * n_blocks + b]
        nxt = soff_ref[e * n_blocks + jnp.minimum(b + 1, n_blocks - 1)]
        return lo, jnp.where(b + 1 < n_blocks, nxt, cap)

    def row_copy(e, j):
        tok = idx_ref[e * cap + j]
        return pltpu.make_async_copy(y_hbm.at[e, pl.ds(j, 1), :],
                                     stage.at[e, pl.ds(tok - b * tt, 1), :], sem.at[e // per_group])

    for grp in range(n_groups):
        first = grp * per_group
        stage[first:first + per_group] = jnp.zeros((per_group,) + stage.shape[1:], stage.dtype)
        for e in range(first, first + per_group):
            lo, hi = slot_range(e)

            def go(j, carry, e=e):
                row_copy(e, j).start()
                return carry
            lax.fori_loop(lo, hi, go, 0)

    for e in range(n_experts):
        lo, hi = slot_range(e)

        def done(j, carry, e=e):
            row_copy(e, j).wait()
            return carry
        lax.fori_loop(lo, hi, done, 0)

    f_lo = jnp.zeros((tt, half), F32)
    f_hi = jnp.zeros((tt, half), F32)
    for e in range(n_experts):
        lo, hi = _unpack_bf16_pair(stage[e])
        f_lo = f_lo + lo
        f_hi = f_hi + hi
    x = x_ref[...]
    x_lo = x[:, :half] + g2_ref[:, :half] * f_lo
    x_hi = x[:, half:] + g2_ref[:, half:] * f_hi
    ms = (jnp.sum(x_lo * x_lo, axis=-1, keepdims=True)
          + jnp.sum(x_hi * x_hi, axis=-1, keepdims=True)) * (1.0 / d)
    inv = lax.rsqrt(ms + EPS)
    o_ref[:, :half] = x_lo * inv * fg_ref[:, :half]
    o_ref[:, half:] = x_hi * inv * fg_ref[:, half:]


COMBINE_GROUPS = 4


def combine_final_norm(idx_flat, soff_flat, x, y_packed, gate, final_g, cap):
    t, d = x.shape
    n_experts = y_packed.shape[0]
    tt = LANE
    n_blocks = t // tt
    vec = pl.BlockSpec((1, d), lambda i, a, b: (0, 0))
    row = pl.BlockSpec((tt, d), lambda i, a, b: (i, 0))
    kern = functools.partial(_combine_kernel, cap=cap, n_blocks=n_blocks, n_experts=n_experts)
    return pl.pallas_call(
        kern,
        grid_spec=pltpu.PrefetchScalarGridSpec(
            num_scalar_prefetch=2,
            grid=(n_blocks,),
            in_specs=[row, pl.BlockSpec(memory_space=pl.ANY), vec, vec],
            out_specs=row,
            scratch_shapes=[pltpu.VMEM((n_experts, tt, d // 2), jnp.int32),
                            pltpu.SemaphoreType.DMA((COMBINE_GROUPS if n_experts % COMBINE_GROUPS == 0 else 1,))],
        ),
        out_shape=jax.ShapeDtypeStruct((t, d), F32),
        compiler_params=_params(("arbitrary",), 48),
        name="combine_final_norm",
    )(idx_flat, soff_flat, x, y_packed, gate, final_g)


def kernel(x, c, ctx, c_ctx, w_ada, b_ada, norm1_g, w_in, b_in, q_norm_g, k_norm_g, w_attn_o,
           dw_w, dw_b, conv_ln_g, conv_ln_b, w_conv_o, w_out, norm2_g, w_router, w_gate, w_up,
           w_down, final_g):
    batch, t, d = x.shape
    depth = w_ada.shape[0]
    assert batch == 1 and depth == 1
    q_w = w_attn_o.shape[1]
    conv_w = w_conv_o.shape[1]
    in_w = w_in.shape[2]
    kv_w = (in_w - q_w - 2 * conv_w - 2 * d) // 2
    n_kv = kv_w // HEAD_DIM
    n_experts = w_router.shape[2]
    cap = CAPACITY_FACTOR * t // n_experts
    o1, o2, o3, o4 = q_w, q_w + kv_w, q_w + 2 * kv_w, q_w + 2 * kv_w + 2 * conv_w

    row = lambda v: v.reshape(1, -1)
    x2, ctx2 = x[0], ctx[0]

    cvecs_t = jnp.stack([c[0], c_ctx], axis=1)
    mod = ada_modulation(cvecs_t, w_ada[0], row(b_ada[0]))
    sh1, sc1, g1, sh2, sc2, g2 = [mod[0:1, i * d:(i + 1) * d] for i in range(N_MOD)]
    csh1, csc1 = mod[1:2, 0:d], mod[1:2, d:2 * d]

    h_lat = norm_modulate(x2, row(norm1_g[0]), sc1, sh1)
    h_ctx = norm_modulate(ctx2, row(norm1_g[0]), csc1, csh1)

    w_in0, b_in0 = w_in[0], row(b_in[0])
    w_qkv = w_in0[:, :o3].astype(BF16)
    cosf, sins = rope_tables(t)
    qg, kg = row(q_norm_g[0]), row(k_norm_g[0])
    gw = q_w // n_kv
    assert gw == kv_w
    q_scale = HEAD_DIM ** -0.5 * LOG2E
    qkv = qkv_projection(h_lat, w_qkv, b_in0[:, :o3], qg, kg, cosf, sins,
                         block_w=gw, n_q_blocks=n_kv, q_scale=q_scale)
    u = matmul_bias(h_lat, w_in0[:, o3:o4].astype(BF16), b_in0[:, o3:o4])
    gates = matmul_bias(h_lat, w_in0[:, o4:].astype(BF16), b_in0[:, o4:])
    kv_ctx = matmul_bias(h_ctx, w_qkv[:, o1:o3], b_in0[:, o1:o3])

    tc = ctx2.shape[0]
    k_ctx = head_norm(kv_ctx, 0, 1, kv_w, kg, cosf[:tc], sins[:tc], rope=False, scale=1.0)
    k_all = jnp.concatenate([k_ctx, qkv[:, o1:o2]], axis=0)
    v_all = jnp.concatenate([kv_ctx[:, kv_w:], qkv[:, o2:o3]], axis=0)
    score_bound = (1.02 * HEAD_DIM * q_scale) * jnp.max(jnp.abs(q_norm_g[0])) * jnp.max(jnp.abs(k_norm_g[0]))
    attn = gqa_attention(qkv, q_w, k_all, v_all, n_kv, score_bound)

    conv = conformer_conv_pre(u, dw_w[0], row(dw_b[0]), row(conv_ln_g[0]), row(conv_ln_b[0]))
    merged = gated_merge(attn, w_attn_o[0].astype(BF16), conv, w_conv_o[0].astype(BF16), gates)
    x1 = matmul_gated_residual(merged, w_out[0].astype(BF16), x2, g1)

    h2p, aff = norm_router(x1, row(norm2_g[0]), sc2, sh2, w_router[0])
    aff3 = aff[:, :n_experts].T.reshape(n_experts, t // LANE, LANE)
    idx, vals, soff = expert_select(aff3, cap)
    idx_flat = idx.reshape(-1)
    hid = expert_up(idx_flat, h2p, w_gate[0], w_up[0], cap)
    y = expert_down(hid, w_down[0], vals.reshape(n_experts, cap, 1))
    out = combine_final_norm(idx_flat, soff.reshape(-1), x1, y, g2, row(final_g), cap)
    return out[None]
```

```python
import functools

import jax
import jax.numpy as jnp
from jax import lax
from jax.experimental import pallas as pl
from jax.experimental.pallas import tpu as pltpu

F32 = jnp.float32
BF16 = jnp.bfloat16

HEAD_DIM = 128
GRID_W = 64
ROPE_THETA = 10000.0
ROPE_AXIS_DIM = HEAD_DIM // 2
EPS = 1e-6
CAPACITY_FACTOR = 2
N_MOD = 6
HALO = 16
LANE = 128
SUBLANES = 8
MIB = 1024 * 1024
LOG2E = 1.4426950408889634


def _tile(n, pref, mult=LANE):
    if n <= pref:
        return n
    t = (pref // mult) * mult
    while t >= mult:
        if n % t == 0:
            return t
        t -= mult
    return n


def _params(sem, vmem_mib):
    return pltpu.CompilerParams(dimension_semantics=sem, vmem_limit_bytes=vmem_mib * MIB)


def _ada_kernel(s_ref, w_ref, b_ref, o_ref):
    k = pl.program_id(1)

    @pl.when(k == 0)
    def _():
        o_ref[...] = jnp.broadcast_to(b_ref[...], o_ref.shape)

    s = s_ref[...]
    s = s * jax.nn.sigmoid(s)
    w = w_ref[...]
    r0 = jnp.sum(s[:, 0:1] * w, axis=0, keepdims=True)
    r1 = jnp.sum(s[:, 1:2] * w, axis=0, keepdims=True)
    o_ref[0:1, :] += r0
    o_ref[1:2, :] += r1


def ada_modulation(cvecs_t, w_ada, b_ada):
    d, n = w_ada.shape
    tk = _tile(d, 1024, 8)
    tn = _tile(n, 2048)
    return pl.pallas_call(
        _ada_kernel,
        grid=(n // tn, d // tk),
        in_specs=[
            pl.BlockSpec((tk, 2), lambda j, k: (k, 0)),
            pl.BlockSpec((tk, tn), lambda j, k: (k, j)),
            pl.BlockSpec((1, tn), lambda j, k: (0, j)),
        ],
        out_specs=pl.BlockSpec((2, tn), lambda j, k: (0, j)),
        out_shape=jax.ShapeDtypeStruct((2, n), F32),
        compiler_params=_params(("parallel", "arbitrary"), 48),
        name="ada_modulation",
    )(cvecs_t, w_ada, b_ada)


def _normmod_kernel(x_ref, g_ref, sc_ref, sh_ref, o_ref):
    x = x_ref[...].astype(F32)
    ms = jnp.mean(x * x, axis=-1, keepdims=True)
    y = x * lax.rsqrt(ms + EPS) * g_ref[...]
    o_ref[...] = (y * (1.0 + sc_ref[...]) + sh_ref[...]).astype(o_ref.dtype)


def norm_modulate(x, g, scale, shift, out_dtype=BF16):
    t, d = x.shape
    tm = _tile(t, 256, 8)
    vec = pl.BlockSpec((1, d), lambda i: (0, 0))
    return pl.pallas_call(
        _normmod_kernel,
        grid=(t // tm,),
        in_specs=[pl.BlockSpec((tm, d), lambda i: (i, 0)), vec, vec, vec],
        out_specs=pl.BlockSpec((tm, d), lambda i: (i, 0)),
        out_shape=jax.ShapeDtypeStruct((t, d), out_dtype),
        compiler_params=_params(("parallel",), 40),
        name="norm_modulate",
    )(x, g, scale, shift)


W_COLS = 512


def _mm_bias_kernel(a_ref, w_ref, b_ref, o_ref, w_sc):
    @pl.when(pl.program_id(1) == 0)
    def _():
        w_sc[...] = w_ref[...].astype(BF16)

    acc = jnp.dot(a_ref[...], w_sc[...], preferred_element_type=F32)
    o_ref[...] = (acc + b_ref[...]).astype(o_ref.dtype)


def matmul_bias(a, w, b, col0, n, out_dtype=BF16):
    m, k = a.shape
    tm = _tile(m, 1024, 8)
    tn = _tile(n, W_COLS)
    assert col0 % tn == 0
    c0 = col0 // tn
    return pl.pallas_call(
        _mm_bias_kernel,
        grid=(n // tn, m // tm),
        in_specs=[
            pl.BlockSpec((tm, k), lambda j, i: (i, 0)),
            pl.BlockSpec((k, tn), lambda j, i: (0, c0 + j)),
            pl.BlockSpec((1, tn), lambda j, i: (0, c0 + j)),
        ],
        out_specs=pl.BlockSpec((tm, tn), lambda j, i: (i, j)),
        out_shape=jax.ShapeDtypeStruct((m, n), out_dtype),
        scratch_shapes=[pltpu.VMEM((k, tn), BF16)],
        compiler_params=_params(("parallel", "arbitrary"), 56),
        name="matmul_bias",
    )(a, w, b)


def _norm_heads(x, g, o_ref, *, scale, cosf=None, sins=None):
    if cosf is not None:
        lane = lax.broadcasted_iota(jnp.int32, cosf.shape, 1)
        first_half = (lane % (ROPE_AXIS_DIM)) < (ROPE_AXIS_DIM // 2)
    for h in range(x.shape[1] // HEAD_DIM):
        xh = x[:, h * HEAD_DIM:(h + 1) * HEAD_DIM]
        ms = jnp.mean(xh * xh, axis=-1, keepdims=True)
        y = xh * lax.rsqrt(ms + EPS) * g
        if cosf is not None:
            up = pltpu.roll(y, HEAD_DIM - ROPE_AXIS_DIM // 2, axis=1)
            dn = pltpu.roll(y, ROPE_AXIS_DIM // 2, axis=1)
            y = y * cosf + jnp.where(first_half, up, dn) * sins
        if scale != 1.0:
            y = y * scale
        o_ref[:, h * HEAD_DIM:(h + 1) * HEAD_DIM] = y.astype(o_ref.dtype)


def _headnorm_kernel(x_ref, g_ref, cos_ref, sin_ref, o_ref, *, n_heads, rope, scale):
    x = x_ref[...].astype(F32)
    if rope:
        _norm_heads(x, g_ref[...], o_ref, scale=scale, cosf=cos_ref[...], sins=sin_ref[...])
    else:
        _norm_heads(x, g_ref[...], o_ref, scale=scale)


def _qkv_kernel(a_ref, w_ref, b_ref, qg_ref, kg_ref, cos_ref, sin_ref, o_ref, w_sc,
                *, n_q_blocks, q_scale, row_chunk):
    j = pl.program_id(0)

    @pl.when(pl.program_id(1) == 0)
    def _():
        w_sc[...] = w_ref[...].astype(BF16)

    is_q = j < n_q_blocks
    is_v = j > n_q_blocks
    g = jnp.where(is_q, qg_ref[...], kg_ref[...])
    scale = jnp.where(is_q, jnp.float32(q_scale), jnp.float32(1.0))
    w = w_sc[...]
    bias = b_ref[...]
    lane = lax.broadcasted_iota(jnp.int32, (row_chunk, HEAD_DIM), 1)
    first_half = (lane % (ROPE_AXIS_DIM)) < (ROPE_AXIS_DIM // 2)
    for r0 in range(0, a_ref.shape[0], row_chunk):
        rows = slice(r0, r0 + row_chunk)
        acc = jnp.dot(a_ref[rows, :], w, preferred_element_type=F32) + bias
        cosf, sins = cos_ref[rows, :], sin_ref[rows, :]
        for h in range(w.shape[1] // HEAD_DIM):
            cols = slice(h * HEAD_DIM, (h + 1) * HEAD_DIM)
            xh = acc[:, cols]
            ms = jnp.mean(xh * xh, axis=-1, keepdims=True)
            y = xh * lax.rsqrt(ms + EPS) * g
            up = pltpu.roll(y, HEAD_DIM - ROPE_AXIS_DIM // 2, axis=1)
            dn = pltpu.roll(y, ROPE_AXIS_DIM // 2, axis=1)
            y = (y * cosf + jnp.where(first_half, up, dn) * sins) * scale
            o_ref[rows, cols] = jnp.where(is_v, xh, y).astype(o_ref.dtype)


def qkv_projection(a, w, b, qg, kg, cos, sin, *, block_w, n_q_blocks, q_scale):
    m, k = a.shape
    n = (n_q_blocks + 2) * block_w
    tm = _tile(m, 1024, 8)
    kern = functools.partial(_qkv_kernel, n_q_blocks=n_q_blocks, q_scale=q_scale,
                             row_chunk=_tile(tm, 256, 8))
    gain = pl.BlockSpec((1, HEAD_DIM), lambda j, i: (0, 0))
    tab = pl.BlockSpec((tm, HEAD_DIM), lambda j, i: (i, 0))
    return pl.pallas_call(
        kern,
        grid=(n // block_w, m // tm),
        in_specs=[
            pl.BlockSpec((tm, k), lambda j, i: (i, 0)),
            pl.BlockSpec((k, block_w), lambda j, i: (0, j)),
            pl.BlockSpec((1, block_w), lambda j, i: (0, j)),
            gain, gain, tab, tab,
        ],
        out_specs=pl.BlockSpec((tm, block_w), lambda j, i: (i, j)),
        out_shape=jax.ShapeDtypeStruct((m, n), BF16),
        scratch_shapes=[pltpu.VMEM((k, block_w), BF16)],
        compiler_params=_params(("parallel", "arbitrary"), 56),
        name="qkv_projection",
    )(a, w, b, qg, kg, cos, sin)


def head_norm(x, col_block, n_blocks, width, g, cos, sin, *, rope, scale):
    t = x.shape[0]
    tm = _tile(t, 512, 8)
    n_heads = width // HEAD_DIM
    kern = functools.partial(_headnorm_kernel, n_heads=n_heads, rope=rope, scale=scale)
    tab = pl.BlockSpec((tm, HEAD_DIM), lambda i, j: (i, 0))
    return pl.pallas_call(
        kern,
        grid=(t // tm, n_blocks),
        in_specs=[
            pl.BlockSpec((tm, width), lambda i, j: (i, col_block + j)),
            pl.BlockSpec((1, HEAD_DIM), lambda i, j: (0, 0)),
            tab, tab,
        ],
        out_specs=pl.BlockSpec((tm, width), lambda i, j: (i, j)),
        out_shape=jax.ShapeDtypeStruct((t, n_blocks * width), BF16),
        compiler_params=_params(("parallel", "parallel"), 32),
        name="head_norm_rope",
    )(x, g, cos, sin)


def rope_tables(n_tokens):
    n_rows = n_tokens // GRID_W
    row = jnp.repeat(jnp.arange(n_rows, dtype=jnp.int32), GRID_W)
    col = jnp.tile(jnp.arange(GRID_W, dtype=jnp.int32), n_rows)
    pos = jnp.stack([row, col], axis=-1).astype(F32)
    inv_freq = ROPE_THETA ** (-jnp.arange(0, ROPE_AXIS_DIM, 2, dtype=F32) / ROPE_AXIS_DIM)
    ang = pos[:, :, None] * inv_freq
    cos, sin = jnp.cos(ang), jnp.sin(ang)
    cosf = jnp.concatenate([cos, cos], axis=-1).reshape(n_tokens, HEAD_DIM)
    sins = jnp.concatenate([-sin, sin], axis=-1).reshape(n_tokens, HEAD_DIM)
    return cosf, sins


def _attn_kernel(b_ref, q_ref, k_ref, vt_ref, o_ref, acc_sc, *, group, n_chunks, bounded):
    tq = q_ref.shape[0]
    acc_sc[...] = jnp.zeros_like(acc_sc)
    qs = [q_ref[:, g * HEAD_DIM:(g + 1) * HEAD_DIM] for g in range(group)]
    shift = b_ref[0, 0]

    def body(c, carry):
        ms, ls = carry
        kc = k_ref[0, c]
        vc = vt_ref[0, c]
        new_m, new_l = [], []
        for g in range(group):
            s = lax.dot_general(kc, qs[g], (((1,), (1,)), ((), ())),
                                preferred_element_type=F32)
            if bounded:
                p = jnp.exp2(s - shift)
                new_m.append(ms[g])
                new_l.append(ls[g] + jnp.sum(p, axis=0, keepdims=True))
                acc_sc[g] += jnp.dot(vc, p.astype(BF16), preferred_element_type=F32)
            else:
                m_new = jnp.maximum(ms[g], jnp.max(s, axis=0, keepdims=True))
                p = jnp.exp2(s - m_new)
                alpha = jnp.exp2(ms[g] - m_new)
                new_m.append(m_new)
                new_l.append(alpha * ls[g] + jnp.sum(p, axis=0, keepdims=True))
                pv = jnp.dot(vc, p.astype(BF16), preferred_element_type=F32)
                acc_sc[g] = alpha * acc_sc[g] + pv
        return tuple(new_m), tuple(new_l)

    m0 = tuple(jnp.full((1, tq), -jnp.inf, F32) for _ in range(group))
    l0 = tuple(jnp.zeros((1, tq), F32) for _ in range(group))
    _, ls = lax.fori_loop(0, n_chunks, body, (m0, l0))
    for g in range(group):
        out_t = acc_sc[g] / ls[g]
        o_ref[:, g * HEAD_DIM:(g + 1) * HEAD_DIM] = out_t.T.astype(o_ref.dtype)


MAX_SAFE_SCORE_BOUND = 60.0


def gqa_attention(q, qw, k_all, v_all, n_kv_heads, score_bound):
    t = q.shape[0]
    tk_all = k_all.shape[0]
    group = qw // HEAD_DIM // n_kv_heads
    gw = group * HEAD_DIM
    tq = _tile(t, 512)
    tkc = _tile(tk_all, 1280)
    nch = tk_all // tkc
    k4 = k_all.reshape(nch, tkc, n_kv_heads, HEAD_DIM).transpose(2, 0, 1, 3)
    vt4 = v_all.reshape(nch, tkc, n_kv_heads, HEAD_DIM).transpose(2, 0, 3, 1)
    shift = score_bound.reshape(1, 1).astype(F32)

    def call(bounded):
        kern = functools.partial(_attn_kernel, group=group, n_chunks=nch, bounded=bounded)
        return pl.pallas_call(
            kern,
            grid=(n_kv_heads, t // tq),
            in_specs=[
                pl.BlockSpec(memory_space=pltpu.SMEM),
                pl.BlockSpec((tq, gw), lambda h, i: (i, h)),
                pl.BlockSpec((1, nch, tkc, HEAD_DIM), lambda h, i: (h, 0, 0, 0)),
                pl.BlockSpec((1, nch, HEAD_DIM, tkc), lambda h, i: (h, 0, 0, 0)),
            ],
            out_specs=pl.BlockSpec((tq, gw), lambda h, i: (i, h)),
            out_shape=jax.ShapeDtypeStruct((t, qw), BF16),
            scratch_shapes=[pltpu.VMEM((group, HEAD_DIM, tq), F32)],
            compiler_params=_params(("parallel", "parallel"), 48),
            name="gqa_flash_attention_bounded" if bounded else "gqa_flash_attention_online",
        )(shift, q, k4, vt4)

    return lax.cond(score_bound < MAX_SAFE_SCORE_BOUND, lambda: call(True), lambda: call(False))


def _conv_kernel(a_ref, g_ref, ap_ref, gp_ref, an_ref, gn_ref, w_ref, b_ref, lg_ref, lb_ref,
                 o_ref, y_sc, c_sc, sh_sc, *, ksize, rows, lanes):
    i = pl.program_id(0)
    n = pl.num_programs(0)
    tt, cw = o_ref.shape

    def glu(a, g):
        a = a.astype(F32)
        g = g.astype(F32)
        return a * jax.nn.sigmoid(g)

    y_sc[HALO:HALO + tt, :] = glu(a_ref[...], g_ref[...])
    y_sc[0:HALO, :] = jnp.where(i > 0, glu(ap_ref[...], gp_ref[...]), 0.0)
    y_sc[HALO + tt:HALO + tt + HALO, :] = jnp.where(i < n - 1, glu(an_ref[...], gn_ref[...]), 0.0)

    base = HALO - ksize // 2
    span = sh_sc.shape[1]
    for c0 in range(0, cw, lanes):
        wc = w_ref[:, c0:c0 + lanes]
        bc = b_ref[:, c0:c0 + lanes]
        for s in range(SUBLANES):
            sh_sc[s] = y_sc[s:s + span, c0:c0 + lanes]
        for r0 in range(0, tt, rows):
            acc = jnp.broadcast_to(bc, (rows, lanes))
            for j in range(ksize):
                s = (base + j) % SUBLANES
                top = r0 + base + j - s
                acc = acc + wc[j:j + 1, :] * sh_sc[s, top:top + rows, :]
            c_sc[r0:r0 + rows, c0:c0 + lanes] = acc

    y = c_sc[...]
    mu = jnp.mean(y, axis=-1, keepdims=True)
    yc = y - mu
    var = jnp.mean(yc * yc, axis=-1, keepdims=True)
    z = yc * lax.rsqrt(var + EPS) * lg_ref[...] + lb_ref[...]
    o_ref[...] = (z * jax.nn.sigmoid(z)).astype(o_ref.dtype)


def conformer_conv_pre(u, dw_w, dw_b, ln_g, ln_b):
    t, c2 = u.shape
    cw = c2 // 2
    ksize = dw_w.shape[0]
    assert ksize // 2 <= HALO - 1
    tt = _tile(t, 256, HALO)
    hb = tt // HALO
    n_hb = t // HALO
    cur_a = pl.BlockSpec((tt, cw), lambda i: (i, 0))
    cur_g = pl.BlockSpec((tt, cw), lambda i: (i, 1))
    prev_a = pl.BlockSpec((HALO, cw), lambda i: (jnp.maximum(i * hb - 1, 0), 0))
    prev_g = pl.BlockSpec((HALO, cw), lambda i: (jnp.maximum(i * hb - 1, 0), 1))
    next_a = pl.BlockSpec((HALO, cw), lambda i: (jnp.minimum((i + 1) * hb, n_hb - 1), 0))
    next_g = pl.BlockSpec((HALO, cw), lambda i: (jnp.minimum((i + 1) * hb, n_hb - 1), 1))
    vec = pl.BlockSpec((1, cw), lambda i: (0, 0))
    lanes = min(512, cw)
    kern = functools.partial(_conv_kernel, ksize=ksize, rows=min(64, tt), lanes=lanes)
    return pl.pallas_call(
        kern,
        grid=(t // tt,),
        in_specs=[cur_a, cur_g, prev_a, prev_g, next_a, next_g,
                  pl.BlockSpec((ksize, cw), lambda i: (0, 0)), vec, vec, vec],
        out_specs=pl.BlockSpec((tt, cw), lambda i: (i, 0)),
        out_shape=jax.ShapeDtypeStruct((t, cw), BF16),
        scratch_shapes=[pltpu.VMEM((tt + 2 * HALO, cw), F32), pltpu.VMEM((tt, cw), F32),
                        pltpu.VMEM((SUBLANES, tt + 2 * HALO - SUBLANES, lanes), F32)],
        compiler_params=_params(("parallel",), 40),
        name="conformer_conv",
    )(u, u, u, u, u, u, dw_w, dw_b, ln_g, ln_b)


def _merge_kernel(a_ref, wa_ref, b_ref, wb_ref, ga_ref, gb_ref, o_ref):
    pa = jnp.dot(a_ref[...], wa_ref[...], preferred_element_type=F32)
    pb = jnp.dot(b_ref[...], wb_ref[...], preferred_element_type=F32)
    ga = jax.nn.sigmoid(ga_ref[...].astype(F32))
    gb = jax.nn.sigmoid(gb_ref[...].astype(F32))
    o_ref[...] = (ga * pa + gb * pb).astype(o_ref.dtype)


def gated_merge(attn, w_attn_o, conv, w_conv_o, gates):
    t, ka = attn.shape
    kb = conv.shape[1]
    d = w_attn_o.shape[1]
    tm = _tile(t, 512, 8)
    tn = _tile(d, 1024)
    nb = d // tn
    return pl.pallas_call(
        _merge_kernel,
        grid=(nb, t // tm),
        in_specs=[
            pl.BlockSpec((tm, ka), lambda j, i: (i, 0)),
            pl.BlockSpec((ka, tn), lambda j, i: (0, j)),
            pl.BlockSpec((tm, kb), lambda j, i: (i, 0)),
            pl.BlockSpec((kb, tn), lambda j, i: (0, j)),
            pl.BlockSpec((tm, tn), lambda j, i: (i, j)),
            pl.BlockSpec((tm, tn), lambda j, i: (i, nb + j)),
        ],
        out_specs=pl.BlockSpec((tm, tn), lambda j, i: (i, j)),
        out_shape=jax.ShapeDtypeStruct((t, d), BF16),
        compiler_params=_params(("parallel", "parallel"), 48),
        name="gated_merge",
    )(attn, w_attn_o, conv, w_conv_o, gates, gates)


def _mm_resid_kernel(a_ref, w_ref, x_ref, g_ref, o_ref):
    acc = jnp.dot(a_ref[...], w_ref[...], preferred_element_type=F32)
    o_ref[...] = x_ref[...] + g_ref[...] * acc


def matmul_gated_residual(a, w, x, gate):
    m, k = a.shape
    n = w.shape[1]
    tm = _tile(m, 1024, 8)
    tn = _tile(n, 1024)
    return pl.pallas_call(
        _mm_resid_kernel,
        grid=(n // tn, m // tm),
        in_specs=[
            pl.BlockSpec((tm, k), lambda j, i: (i, 0)),
            pl.BlockSpec((k, tn), lambda j, i: (0, j)),
            pl.BlockSpec((tm, tn), lambda j, i: (i, j)),
            pl.BlockSpec((1, tn), lambda j, i: (0, j)),
        ],
        out_specs=pl.BlockSpec((tm, tn), lambda j, i: (i, j)),
        out_shape=jax.ShapeDtypeStruct((m, n), F32),
        compiler_params=_params(("parallel", "parallel"), 56),
        name="out_proj_residual",
    )(a, w, x, gate)


HI_HALF_MASK = -65536


def _pack_bf16_pair(lo, hi):
    lo_b = lax.bitcast_convert_type(lo.astype(BF16).astype(F32), jnp.int32)
    hi_b = lax.bitcast_convert_type(hi.astype(BF16).astype(F32), jnp.int32)
    return (hi_b & HI_HALF_MASK) | lax.shift_right_logical(lo_b, jnp.full(lo_b.shape, 16, jnp.int32))


def _unpack_bf16_pair(w):
    lo = lax.bitcast_convert_type(w << 16, F32)
    hi = lax.bitcast_convert_type(w & HI_HALF_MASK, F32)
    return lo, hi


def _router_kernel(x_ref, g_ref, sc_ref, sh_ref, wh_ref, wl_ref, h_ref, aff_ref, *, n_experts):
    x = x_ref[...]
    half = x.shape[1] // 2
    ms = jnp.mean(x * x, axis=-1, keepdims=True)
    h = x * lax.rsqrt(ms + EPS) * g_ref[...]
    h = h * (1.0 + sc_ref[...]) + sh_ref[...]
    hb = h.astype(BF16)
    h_ref[...] = _pack_bf16_pair(h[:, :half], h[:, half:])
    hl = (h - hb.astype(F32)).astype(BF16)
    logits = (jnp.dot(hb, wh_ref[...], preferred_element_type=F32)
              + jnp.dot(hb, wl_ref[...], preferred_element_type=F32)
              + jnp.dot(hl, wh_ref[...], preferred_element_type=F32))
    lane = lax.broadcasted_iota(jnp.int32, logits.shape, 1)
    logits = jnp.where(lane < n_experts, logits, -jnp.inf)
    mx = jnp.max(logits, axis=-1, keepdims=True)
    e = jnp.exp(logits - mx)
    aff_ref[...] = e / jnp.sum(e, axis=-1, keepdims=True)


def norm_router(x, g, scale, shift, w_router):
    t, d = x.shape
    n_experts = w_router.shape[1]
    wp = jnp.zeros((d, LANE), F32).at[:, :n_experts].set(w_router)
    wh = wp.astype(BF16)
    wl = (wp - wh.astype(F32)).astype(BF16)
    tm = _tile(t, 256, 8)
    vec = pl.BlockSpec((1, d), lambda i: (0, 0))
    wspec = pl.BlockSpec((d, LANE), lambda i: (0, 0))
    kern = functools.partial(_router_kernel, n_experts=n_experts)
    return pl.pallas_call(
        kern,
        grid=(t // tm,),
        in_specs=[pl.BlockSpec((tm, d), lambda i: (i, 0)), vec, vec, vec, wspec, wspec],
        out_specs=[pl.BlockSpec((tm, d // 2), lambda i: (i, 0)), pl.BlockSpec((tm, LANE), lambda i: (i, 0))],
        out_shape=[jax.ShapeDtypeStruct((t, d // 2), jnp.int32), jax.ShapeDtypeStruct((t, LANE), F32)],
        compiler_params=_params(("parallel",), 40),
        name="norm_router",
    )(x, g, scale, shift, wh, wl)


def _block_cumsum(m):
    nb = m.shape[0]
    r = lax.broadcasted_iota(jnp.int32, (LANE, LANE), 0)
    c = lax.broadcasted_iota(jnp.int32, (LANE, LANE), 1)
    upper = jnp.where(r <= c, 1.0, 0.0).astype(BF16)
    inblock = jnp.dot(m.astype(BF16), upper, preferred_element_type=F32)
    tot = jnp.broadcast_to(inblock[:, LANE - 1:LANE], (nb, LANE))
    rb = lax.broadcasted_iota(jnp.int32, (nb, nb), 0)
    cb = lax.broadcasted_iota(jnp.int32, (nb, nb), 1)
    lower = jnp.where(cb < rb, 1.0, 0.0).astype(BF16)
    before = jnp.dot(lower, tot.astype(BF16), preferred_element_type=F32)
    return inblock, before


def _select_kernel(a_ref, idx_ref, val_ref, soff_ref, *, cap):
    a = a_ref[0]
    nb = a.shape[0]
    bits = lax.bitcast_convert_type(a, jnp.int32)
    capf = jnp.float32(cap)

    def count(pred):
        cnt = jnp.sum(jnp.where(pred, 1.0, 0.0), axis=0, keepdims=True)
        return jnp.sum(cnt, axis=1, keepdims=True)

    def bisect(i, thr):
        cand = thr | lax.shift_left(jnp.int32(1), 30 - i)
        return jnp.where(count(bits >= cand) >= capf, cand, thr)

    thr = lax.fori_loop(0, 31, bisect, jnp.zeros((1, 1), jnp.int32))
    gt = bits > thr
    eq = bits == thr
    need = capf - count(gt)
    eqf = jnp.where(eq, 1.0, 0.0)
    eq_in, eq_before = _block_cumsum(eqf)
    take_eq = eq & ((eq_in + eq_before - eqf) < need)
    sel = jnp.where(gt | take_eq, 1.0, 0.0)
    inblock, before = _block_cumsum(sel)

    slot = lax.broadcasted_iota(jnp.int32, (1, cap), 1).astype(F32)
    upto = before[:, 0:1] + inblock[:, LANE - 1:LANE]
    blk = jnp.sum(jnp.where(upto <= slot, 1.0, 0.0), axis=0, keepdims=True)
    bi = lax.broadcasted_iota(jnp.int32, (nb, cap), 0).astype(F32)
    onehot = jnp.where(bi == blk, 1.0, 0.0).astype(BF16)

    rows = jnp.dot(inblock.T.astype(BF16), onehot, preferred_element_type=F32)
    before_t = before.T
    hi = jnp.floor(before_t * (1.0 / LANE))
    lo = before_t - hi * LANE
    start = (jnp.dot(hi.astype(BF16), onehot, preferred_element_type=F32) * LANE
             + jnp.dot(lo.astype(BF16), onehot, preferred_element_type=F32))
    rank = slot - start
    lane_of = jnp.sum(jnp.where(rows <= rank, 1.0, 0.0), axis=0, keepdims=True)
    idx_ref[0] = (blk * LANE + lane_of).astype(jnp.int32)

    at = a.T
    a1 = at.astype(BF16)
    r1 = at - a1.astype(F32)
    a2 = r1.astype(BF16)
    a3 = (r1 - a2.astype(F32)).astype(BF16)
    picked = (jnp.dot(a1, onehot, preferred_element_type=F32)
              + jnp.dot(a2, onehot, preferred_element_type=F32)
              + jnp.dot(a3, onehot, preferred_element_type=F32))
    li = lax.broadcasted_iota(jnp.int32, (LANE, cap), 0).astype(F32)
    val_ref[0] = jnp.sum(jnp.where(li == lane_of, picked, 0.0), axis=0, keepdims=True)
    soff_ref[0] = before_t[0:1, :].astype(jnp.int32)


def expert_select(aff3, cap):
    e, nb, _ = aff3.shape
    kern = functools.partial(_select_kernel, cap=cap)
    return pl.pallas_call(
        kern,
        grid=(e,),
        in_specs=[pl.BlockSpec((1, nb, LANE), lambda i: (i, 0, 0))],
        out_specs=[pl.BlockSpec((1, 1, cap), lambda i: (i, 0, 0)),
                   pl.BlockSpec((1, 1, cap), lambda i: (i, 0, 0)),
                   pl.BlockSpec((1, 1, nb), lambda i: (i, 0, 0))],
        out_shape=[jax.ShapeDtypeStruct((e, 1, cap), jnp.int32),
                   jax.ShapeDtypeStruct((e, 1, cap), F32),
                   jax.ShapeDtypeStruct((e, 1, nb), jnp.int32)],
        compiler_params=_params(("parallel",), 32),
        name="expert_select",
    )(aff3)


def _expert_up_kernel(idx_ref, h_hbm, wg_ref, wu_ref, o_ref, stage, x_sc, sem, *, cap, phase_rows):
    e = pl.program_id(0)
    j = pl.program_id(1)
    half = x_sc.shape[1] // 2
    n_phases = cap // phase_rows
    unroll = SUBLANES
    assert phase_rows % unroll == 0

    def row_copy(p, r):
        tok = idx_ref[e * cap + p * phase_rows + r]
        return pltpu.make_async_copy(h_hbm.at[pl.ds(tok, 1), :],
                                     stage.at[p % 2, pl.ds(r, 1), :], sem.at[p % 2])

    def start_phase(p):
        def go(r, carry):
            for k in range(unroll):
                row_copy(p, r * unroll + k).start()
            return carry
        lax.fori_loop(0, phase_rows // unroll, go, 0)

    def wait_phase(p):
        def go(r, carry):
            for k in range(unroll):
                row_copy(p, r * unroll + k).wait()
            return carry
        lax.fori_loop(0, phase_rows // unroll, go, 0)

    @pl.when(j == 0)
    def _():
        start_phase(0)
        for p in range(n_phases):
            if p + 1 < n_phases:
                start_phase(p + 1)
            wait_phase(p)
            lo, hi = _unpack_bf16_pair(stage[p % 2])
            x_sc[p * phase_rows:(p + 1) * phase_rows, :half] = lo.astype(BF16)
            x_sc[p * phase_rows:(p + 1) * phase_rows, half:] = hi.astype(BF16)

    tf = o_ref.shape[2]
    w = jnp.concatenate([wg_ref[0].astype(BF16), wu_ref[0].astype(BF16)], axis=1)
    gu = jnp.dot(x_sc[...], w, preferred_element_type=F32)
    g, u = gu[:, :tf], gu[:, tf:]
    o_ref[0] = (g * jax.nn.sigmoid(g) * u).astype(o_ref.dtype)


def expert_up(idx_flat, h_packed, w_gate, w_up, cap):
    e, d, f = w_gate.shape
    half = d // 2
    tf = _tile(f, 256)
    phase_rows = _tile(cap, 512, 8)
    wspec = pl.BlockSpec((1, d, tf), lambda i, j, idx: (i, 0, j))
    kern = functools.partial(_expert_up_kernel, cap=cap, phase_rows=phase_rows)
    return pl.pallas_call(
        kern,
        grid_spec=pltpu.PrefetchScalarGridSpec(
            num_scalar_prefetch=1,
            grid=(e, f // tf),
            in_specs=[pl.BlockSpec(memory_space=pl.ANY), wspec, wspec],
            out_specs=pl.BlockSpec((1, cap, tf), lambda i, j, idx: (i, 0, j)),
            scratch_shapes=[pltpu.VMEM((2, phase_rows, half), jnp.int32),
                            pltpu.VMEM((cap, d), BF16),
                            pltpu.SemaphoreType.DMA((2,))],
        ),
        out_shape=jax.ShapeDtypeStruct((e, cap, f), BF16),
        compiler_params=_params(("arbitrary", "arbitrary"), 56),
        name="expert_up_gather",
    )(idx_flat, h_packed, w_gate, w_up)


def _expert_down_kernel(h_ref, wa_ref, wb_ref, v_ref, o_ref):
    h = h_ref[0]
    v = v_ref[0]
    tn = o_ref.shape[2]
    w = jnp.concatenate([wa_ref[0].astype(BF16), wb_ref[0].astype(BF16)], axis=1)
    y = jnp.dot(h, w, preferred_element_type=F32) * v
    o_ref[0] = _pack_bf16_pair(y[:, :tn], y[:, tn:])


def expert_down(hid, w_down, vals):
    e, c, f = hid.shape
    d = w_down.shape[2]
    half = d // 2
    tn = _tile(half, 256)
    nb = half // tn
    return pl.pallas_call(
        _expert_down_kernel,
        grid=(e, nb),
        in_specs=[pl.BlockSpec((1, c, f), lambda i, j: (i, 0, 0)),
                  pl.BlockSpec((1, f, tn), lambda i, j: (i, 0, j)),
                  pl.BlockSpec((1, f, tn), lambda i, j: (i, 0, nb + j)),
                  pl.BlockSpec((1, c, 1), lambda i, j: (i, 0, 0))],
        out_specs=pl.BlockSpec((1, c, tn), lambda i, j: (i, 0, j)),
        out_shape=jax.ShapeDtypeStruct((e, c, half), jnp.int32),
        compiler_params=_params(("parallel", "arbitrary"), 48),
        name="expert_down",
    )(hid, w_down, w_down, vals)


def _combine_kernel(idx_ref, soff_ref, x_ref, y_hbm, g2_ref, fg_ref, o_ref, stage, sem,
                    *, cap, n_blocks, n_experts):
    b = pl.program_id(0)
    tt, d = x_ref.shape
    half = d // 2
    n_groups = sem.shape[0]
    per_group = n_experts // n_groups

    def slot_range(e):
        lo = soff_ref[e * n_blocks + b]
        nxt = soff_ref[e * n_blocks + jnp.minimum(b + 1, n_blocks - 1)]
        return lo, jnp.where(b + 1 < n_blocks, nxt, cap)

    def row_copy(e, j):
        tok = idx_ref[e * cap + j]
        return pltpu.make_async_copy(y_hbm.at[e, pl.ds(j, 1), :],
                                     stage.at[e, pl.ds(tok - b * tt, 1), :], sem.at[e // per_group])

    for grp in range(n_groups):
        first = grp * per_group
        stage[first:first + per_group] = jnp.zeros((per_group,) + stage.shape[1:], stage.dtype)
        for e in range(first, first + per_group):
            lo, hi = slot_range(e)

            def go(j, carry, e=e):
                row_copy(e, j).start()
                return carry
            lax.fori_loop(lo, hi, go, 0)

    for e in range(n_experts):
        lo, hi = slot_range(e)

        def done(j, carry, e=e):
            row_copy(e, j).wait()
            return carry
        lax.fori_loop(lo, hi, done, 0)

    f_lo = jnp.zeros((tt, half), F32)
    f_hi = jnp.zeros((tt, half), F32)
    for e in range(n_experts):
        lo, hi = _unpack_bf16_pair(stage[e])
        f_lo = f_lo + lo
        f_hi = f_hi + hi
    x = x_ref[...]
    x_lo = x[:, :half] + g2_ref[:, :half] * f_lo
    x_hi = x[:, half:] + g2_ref[:, half:] * f_hi
    ms = (jnp.sum(x_lo * x_lo, axis=-1, keepdims=True)
          + jnp.sum(x_hi * x_hi, axis=-1, keepdims=True)) * (1.0 / d)
    inv = lax.rsqrt(ms + EPS)
    o_ref[:, :half] = x_lo * inv * fg_ref[:, :half]
    o_ref[:, half:] = x_hi * inv * fg_ref[:, half:]


COMBINE_GROUPS = 4


def combine_final_norm(idx_flat, soff_flat, x, y_packed, gate, final_g, cap):
    t, d = x.shape
    n_experts = y_packed.shape[0]
    tt = LANE
    n_blocks = t // tt
    vec = pl.BlockSpec((1, d), lambda i, a, b: (0, 0))
    row = pl.BlockSpec((tt, d), lambda i, a, b: (i, 0))
    kern = functools.partial(_combine_kernel, cap=cap, n_blocks=n_blocks, n_experts=n_experts)
    return pl.pallas_call(
        kern,
        grid_spec=pltpu.PrefetchScalarGridSpec(
            num_scalar_prefetch=2,
            grid=(n_blocks,),
            in_specs=[row, pl.BlockSpec(memory_space=pl.ANY), vec, vec],
            out_specs=row,
            scratch_shapes=[pltpu.VMEM((n_experts, tt, d // 2), jnp.int32),
                            pltpu.SemaphoreType.DMA((COMBINE_GROUPS if n_experts % COMBINE_GROUPS == 0 else 1,))],
        ),
        out_shape=jax.ShapeDtypeStruct((t, d), F32),
        compiler_params=_params(("arbitrary",), 48),
        name="combine_final_norm",
    )(idx_flat, soff_flat, x, y_packed, gate, final_g)


def kernel(x, c, ctx, c_ctx, w_ada, b_ada, norm1_g, w_in, b_in, q_norm_g, k_norm_g, w_attn_o,
           dw_w, dw_b, conv_ln_g, conv_ln_b, w_conv_o, w_out, norm2_g, w_router, w_gate, w_up,
           w_down, final_g):
    batch, t, d = x.shape
    depth = w_ada.shape[0]
    assert batch == 1 and depth == 1
    q_w = w_attn_o.shape[1]
    conv_w = w_conv_o.shape[1]
    in_w = w_in.shape[2]
    kv_w = (in_w - q_w - 2 * conv_w - 2 * d) // 2
    n_kv = kv_w // HEAD_DIM
    n_experts = w_router.shape[2]
    cap = CAPACITY_FACTOR * t // n_experts
    o1, o2, o3, o4 = q_w, q_w + kv_w, q_w + 2 * kv_w, q_w + 2 * kv_w + 2 * conv_w

    row = lambda v: v.reshape(1, -1)
    x2, ctx2 = x[0], ctx[0]

    cvecs_t = jnp.stack([c[0], c_ctx], axis=1)
    mod = ada_modulation(cvecs_t, w_ada[0], row(b_ada[0]))
    sh1, sc1, g1, sh2, sc2, g2 = [mod[0:1, i * d:(i + 1) * d] for i in range(N_MOD)]
    csh1, csc1 = mod[1:2, 0:d], mod[1:2, d:2 * d]

    h_lat = norm_modulate(x2, row(norm1_g[0]), sc1, sh1)
    h_ctx = norm_modulate(ctx2, row(norm1_g[0]), csc1, csh1)

    w_in0, b_in0 = w_in[0], row(b_in[0])
    cosf, sins = rope_tables(t)
    qg, kg = row(q_norm_g[0]), row(k_norm_g[0])
    gw = q_w // n_kv
    assert gw == kv_w
    q_scale = HEAD_DIM ** -0.5 * LOG2E
    qkv = qkv_projection(h_lat, w_in0, b_in0, qg, kg, cosf, sins,
                         block_w=gw, n_q_blocks=n_kv, q_scale=q_scale)
    u = matmul_bias(h_lat, w_in0, b_in0, o3, o4 - o3)
    gates = matmul_bias(h_lat, w_in0, b_in0, o4, in_w - o4)
    kv_ctx = matmul_bias(h_ctx, w_in0, b_in0, o1, o3 - o1)

    tc = ctx2.shape[0]
    k_ctx = head_norm(kv_ctx, 0, 1, kv_w, kg, cosf[:tc], sins[:tc], rope=False, scale=1.0)
    k_all = jnp.concatenate([k_ctx, qkv[:, o1:o2]], axis=0)
    v_all = jnp.concatenate([kv_ctx[:, kv_w:], qkv[:, o2:o3]], axis=0)
    score_bound = (1.02 * HEAD_DIM * q_scale) * jnp.max(jnp.abs(q_norm_g[0])) * jnp.max(jnp.abs(k_norm_g[0]))
    attn = gqa_attention(qkv, q_w, k_all, v_all, n_kv, score_bound)

    conv = conformer_conv_pre(u, dw_w[0], row(dw_b[0]), row(conv_ln_g[0]), row(conv_ln_b[0]))
    merged = gated_merge(attn, w_attn_o[0].astype(BF16), conv, w_conv_o[0].astype(BF16), gates)
    x1 = matmul_gated_residual(merged, w_out[0].astype(BF16), x2, g1)

    h2p, aff = norm_router(x1, row(norm2_g[0]), sc2, sh2, w_router[0])
    aff3 = aff[:, :n_experts].T.reshape(n_experts, t // LANE, LANE)
    idx, vals, soff = expert_select(aff3, cap)
    idx_flat = idx.reshape(-1)
    hid = expert_up(idx_flat, h2p, w_gate[0], w_up[0], cap)
    y = expert_down(hid, w_down[0], vals.reshape(n_experts, cap, 1))
    out = combine_final_norm(idx_flat, soff.reshape(-1), x1, y, g2, row(final_g), cap)
    return out[None]
```

```python
import functools

import jax
import jax.numpy as jnp
from jax import lax
from jax.experimental import pallas as pl
from jax.experimental.pallas import tpu as pltpu

F32 = jnp.float32
BF16 = jnp.bfloat16

HEAD_DIM = 128
GRID_W = 64
ROPE_THETA = 10000.0
ROPE_AXIS_DIM = HEAD_DIM // 2
EPS = 1e-6
CAPACITY_FACTOR = 2
N_MOD = 6
HALO = 16
LANE = 128
SUBLANES = 8
MIB = 1024 * 1024
LOG2E = 1.4426950408889634


def _tile(n, pref, mult=LANE):
    if n <= pref:
        return n
    t = (pref // mult) * mult
    while t >= mult:
        if n % t == 0:
            return t
        t -= mult
    return n


def _params(sem, vmem_mib):
    return pltpu.CompilerParams(dimension_semantics=sem, vmem_limit_bytes=vmem_mib * MIB)


def _ada_kernel(s_ref, w_ref, b_ref, o_ref):
    k = pl.program_id(1)

    @pl.when(k == 0)
    def _():
        o_ref[...] = jnp.broadcast_to(b_ref[...], o_ref.shape)

    s = s_ref[...]
    s = s * jax.nn.sigmoid(s)
    w = w_ref[...]
    r0 = jnp.sum(s[:, 0:1] * w, axis=0, keepdims=True)
    r1 = jnp.sum(s[:, 1:2] * w, axis=0, keepdims=True)
    o_ref[0:1, :] += r0
    o_ref[1:2, :] += r1


def ada_modulation(cvecs_t, w_ada, b_ada):
    d, n = w_ada.shape
    tk = _tile(d, 1024, 8)
    tn = _tile(n, 2048)
    return pl.pallas_call(
        _ada_kernel,
        grid=(n // tn, d // tk),
        in_specs=[
            pl.BlockSpec((tk, 2), lambda j, k: (k, 0)),
            pl.BlockSpec((tk, tn), lambda j, k: (k, j)),
            pl.BlockSpec((1, tn), lambda j, k: (0, j)),
        ],
        out_specs=pl.BlockSpec((2, tn), lambda j, k: (0, j)),
        out_shape=jax.ShapeDtypeStruct((2, n), F32),
        compiler_params=_params(("parallel", "arbitrary"), 48),
        name="ada_modulation",
    )(cvecs_t, w_ada, b_ada)


def _normmod_kernel(x_ref, g_ref, sc_ref, sh_ref, o_ref):
    x = x_ref[...].astype(F32)
    ms = jnp.mean(x * x, axis=-1, keepdims=True)
    y = x * lax.rsqrt(ms + EPS) * g_ref[...]
    o_ref[...] = (y * (1.0 + sc_ref[...]) + sh_ref[...]).astype(o_ref.dtype)


def norm_modulate(x, g, scale, shift, out_dtype=BF16):
    t, d = x.shape
    tm = _tile(t, 256, 8)
    vec = pl.BlockSpec((1, d), lambda i: (0, 0))
    return pl.pallas_call(
        _normmod_kernel,
        grid=(t // tm,),
        in_specs=[pl.BlockSpec((tm, d), lambda i: (i, 0)), vec, vec, vec],
        out_specs=pl.BlockSpec((tm, d), lambda i: (i, 0)),
        out_shape=jax.ShapeDtypeStruct((t, d), out_dtype),
        compiler_params=_params(("parallel",), 40),
        name="norm_modulate",
    )(x, g, scale, shift)


W_COLS = 1024


def _mm_bias_kernel(a_ref, w_ref, b_ref, o_ref, w_sc):
    @pl.when(pl.program_id(1) == 0)
    def _():
        w_sc[...] = w_ref[...].astype(BF16)

    acc = jnp.dot(a_ref[...], w_sc[...], preferred_element_type=F32)
    o_ref[...] = (acc + b_ref[...]).astype(o_ref.dtype)


def matmul_bias(a, w, b, col0, n, out_dtype=BF16):
    m, k = a.shape
    tm = _tile(m, 1024, 8)
    tn = _tile(n, W_COLS)
    assert col0 % tn == 0
    c0 = col0 // tn
    return pl.pallas_call(
        _mm_bias_kernel,
        grid=(n // tn, m // tm),
        in_specs=[
            pl.BlockSpec((tm, k), lambda j, i: (i, 0)),
            pl.BlockSpec((k, tn), lambda j, i: (0, c0 + j), pipeline_mode=pl.Buffered(1)),
            pl.BlockSpec((1, tn), lambda j, i: (0, c0 + j)),
        ],
        out_specs=pl.BlockSpec((tm, tn), lambda j, i: (i, j)),
        out_shape=jax.ShapeDtypeStruct((m, n), out_dtype),
        scratch_shapes=[pltpu.VMEM((k, tn), BF16)],
        compiler_params=_params(("parallel", "arbitrary"), 56),
        name="matmul_bias",
    )(a, w, b)


def _norm_heads(x, g, o_ref, *, scale, cosf=None, sins=None):
    if cosf is not None:
        lane = lax.broadcasted_iota(jnp.int32, cosf.shape, 1)
        first_half = (lane % (ROPE_AXIS_DIM)) < (ROPE_AXIS_DIM // 2)
    for h in range(x.shape[1] // HEAD_DIM):
        xh = x[:, h * HEAD_DIM:(h + 1) * HEAD_DIM]
        ms = jnp.mean(xh * xh, axis=-1, keepdims=True)
        y = xh * lax.rsqrt(ms + EPS) * g
        if cosf is not None:
            up = pltpu.roll(y, HEAD_DIM - ROPE_AXIS_DIM // 2, axis=1)
            dn = pltpu.roll(y, ROPE_AXIS_DIM // 2, axis=1)
            y = y * cosf + jnp.where(first_half, up, dn) * sins
        if scale != 1.0:
            y = y * scale
        o_ref[:, h * HEAD_DIM:(h + 1) * HEAD_DIM] = y.astype(o_ref.dtype)


def _headnorm_kernel(x_ref, g_ref, cos_ref, sin_ref, o_ref, *, n_heads, rope, scale):
    x = x_ref[...].astype(F32)
    if rope:
        _norm_heads(x, g_ref[...], o_ref, scale=scale, cosf=cos_ref[...], sins=sin_ref[...])
    else:
        _norm_heads(x, g_ref[...], o_ref, scale=scale)


def _qkv_kernel(a_ref, w_ref, b_ref, qg_ref, kg_ref, cos_ref, sin_ref, o_ref, w_sc,
                *, n_q_blocks, q_scale, row_chunk):
    j = pl.program_id(0)

    @pl.when(pl.program_id(1) == 0)
    def _():
        w_sc[...] = w_ref[...].astype(BF16)

    is_q = j < n_q_blocks
    is_v = j > n_q_blocks
    g = jnp.where(is_q, qg_ref[...], kg_ref[...])
    scale = jnp.where(is_q, jnp.float32(q_scale), jnp.float32(1.0))
    w = w_sc[...]
    bias = b_ref[...]
    lane = lax.broadcasted_iota(jnp.int32, (row_chunk, HEAD_DIM), 1)
    first_half = (lane % (ROPE_AXIS_DIM)) < (ROPE_AXIS_DIM // 2)
    for r0 in range(0, a_ref.shape[0], row_chunk):
        rows = slice(r0, r0 + row_chunk)
        acc = jnp.dot(a_ref[rows, :], w, preferred_element_type=F32) + bias
        cosf, sins = cos_ref[rows, :], sin_ref[rows, :]
        for h in range(w.shape[1] // HEAD_DIM):
            cols = slice(h * HEAD_DIM, (h + 1) * HEAD_DIM)
            xh = acc[:, cols]
            ms = jnp.mean(xh * xh, axis=-1, keepdims=True)
            y = xh * lax.rsqrt(ms + EPS) * g
            up = pltpu.roll(y, HEAD_DIM - ROPE_AXIS_DIM // 2, axis=1)
            dn = pltpu.roll(y, ROPE_AXIS_DIM // 2, axis=1)
            y = (y * cosf + jnp.where(first_half, up, dn) * sins) * scale
            o_ref[rows, cols] = jnp.where(is_v, xh, y).astype(o_ref.dtype)


def qkv_projection(a, w, b, qg, kg, cos, sin, *, block_w, n_q_blocks, q_scale):
    m, k = a.shape
    n = (n_q_blocks + 2) * block_w
    tm = _tile(m, 1024, 8)
    kern = functools.partial(_qkv_kernel, n_q_blocks=n_q_blocks, q_scale=q_scale,
                             row_chunk=_tile(tm, 256, 8))
    gain = pl.BlockSpec((1, HEAD_DIM), lambda j, i: (0, 0))
    tab = pl.BlockSpec((tm, HEAD_DIM), lambda j, i: (i, 0))
    return pl.pallas_call(
        kern,
        grid=(n // block_w, m // tm),
        in_specs=[
            pl.BlockSpec((tm, k), lambda j, i: (i, 0)),
            pl.BlockSpec((k, block_w), lambda j, i: (0, j)),
            pl.BlockSpec((1, block_w), lambda j, i: (0, j)),
            gain, gain, tab, tab,
        ],
        out_specs=pl.BlockSpec((tm, block_w), lambda j, i: (i, j)),
        out_shape=jax.ShapeDtypeStruct((m, n), BF16),
        scratch_shapes=[pltpu.VMEM((k, block_w), BF16)],
        compiler_params=_params(("parallel", "arbitrary"), 56),
        name="qkv_projection",
    )(a, w, b, qg, kg, cos, sin)


def head_norm(x, col_block, n_blocks, width, g, cos, sin, *, rope, scale):
    t = x.shape[0]
    tm = _tile(t, 512, 8)
    n_heads = width // HEAD_DIM
    kern = functools.partial(_headnorm_kernel, n_heads=n_heads, rope=rope, scale=scale)
    tab = pl.BlockSpec((tm, HEAD_DIM), lambda i, j: (i, 0))
    return pl.pallas_call(
        kern,
        grid=(t // tm, n_blocks),
        in_specs=[
            pl.BlockSpec((tm, width), lambda i, j: (i, col_block + j)),
            pl.BlockSpec((1, HEAD_DIM), lambda i, j: (0, 0)),
            tab, tab,
        ],
        out_specs=pl.BlockSpec((tm, width), lambda i, j: (i, j)),
        out_shape=jax.ShapeDtypeStruct((t, n_blocks * width), BF16),
        compiler_params=_params(("parallel", "parallel"), 32),
        name="head_norm_rope",
    )(x, g, cos, sin)


def rope_tables(n_tokens):
    n_rows = n_tokens // GRID_W
    row = jnp.repeat(jnp.arange(n_rows, dtype=jnp.int32), GRID_W)
    col = jnp.tile(jnp.arange(GRID_W, dtype=jnp.int32), n_rows)
    pos = jnp.stack([row, col], axis=-1).astype(F32)
    inv_freq = ROPE_THETA ** (-jnp.arange(0, ROPE_AXIS_DIM, 2, dtype=F32) / ROPE_AXIS_DIM)
    ang = pos[:, :, None] * inv_freq
    cos, sin = jnp.cos(ang), jnp.sin(ang)
    cosf = jnp.concatenate([cos, cos], axis=-1).reshape(n_tokens, HEAD_DIM)
    sins = jnp.concatenate([-sin, sin], axis=-1).reshape(n_tokens, HEAD_DIM)
    return cosf, sins


def _attn_kernel(b_ref, q_ref, k_ref, vt_ref, o_ref, acc_sc, *, group, n_chunks, bounded):
    tq = q_ref.shape[0]
    acc_sc[...] = jnp.zeros_like(acc_sc)
    qs = [q_ref[:, g * HEAD_DIM:(g + 1) * HEAD_DIM] for g in range(group)]
    shift = b_ref[0, 0]

    def body(c, carry):
        ms, ls = carry
        kc = k_ref[0, c]
        vc = vt_ref[0, c]
        new_m, new_l = [], []
        for g in range(group):
            s = lax.dot_general(kc, qs[g], (((1,), (1,)), ((), ())),
                                preferred_element_type=F32)
            if bounded:
                p = jnp.exp2(s - shift)
                new_m.append(ms[g])
                new_l.append(ls[g] + jnp.sum(p, axis=0, keepdims=True))
                acc_sc[g] += jnp.dot(vc, p.astype(BF16), preferred_element_type=F32)
            else:
                m_new = jnp.maximum(ms[g], jnp.max(s, axis=0, keepdims=True))
                p = jnp.exp2(s - m_new)
                alpha = jnp.exp2(ms[g] - m_new)
                new_m.append(m_new)
                new_l.append(alpha * ls[g] + jnp.sum(p, axis=0, keepdims=True))
                pv = jnp.dot(vc, p.astype(BF16), preferred_element_type=F32)
                acc_sc[g] = alpha * acc_sc[g] + pv
        return tuple(new_m), tuple(new_l)

    m0 = tuple(jnp.full((1, tq), -jnp.inf, F32) for _ in range(group))
    l0 = tuple(jnp.zeros((1, tq), F32) for _ in range(group))
    _, ls = lax.fori_loop(0, n_chunks, body, (m0, l0))
    for g in range(group):
        out_t = acc_sc[g] / ls[g]
        o_ref[:, g * HEAD_DIM:(g + 1) * HEAD_DIM] = out_t.T.astype(o_ref.dtype)


MAX_SAFE_SCORE_BOUND = 60.0


def gqa_attention(q, qw, k_all, v_all, n_kv_heads, score_bound):
    t = q.shape[0]
    tk_all = k_all.shape[0]
    group = qw // HEAD_DIM // n_kv_heads
    gw = group * HEAD_DIM
    tq = _tile(t, 512)
    tkc = _tile(tk_all, 1280)
    nch = tk_all // tkc
    k4 = k_all.reshape(nch, tkc, n_kv_heads, HEAD_DIM).transpose(2, 0, 1, 3)
    vt4 = v_all.reshape(nch, tkc, n_kv_heads, HEAD_DIM).transpose(2, 0, 3, 1)
    shift = score_bound.reshape(1, 1).astype(F32)

    def call(bounded):
        kern = functools.partial(_attn_kernel, group=group, n_chunks=nch, bounded=bounded)
        return pl.pallas_call(
            kern,
            grid=(n_kv_heads, t // tq),
            in_specs=[
                pl.BlockSpec(memory_space=pltpu.SMEM),
                pl.BlockSpec((tq, gw), lambda h, i: (i, h)),
                pl.BlockSpec((1, nch, tkc, HEAD_DIM), lambda h, i: (h, 0, 0, 0)),
                pl.BlockSpec((1, nch, HEAD_DIM, tkc), lambda h, i: (h, 0, 0, 0)),
            ],
            out_specs=pl.BlockSpec((tq, gw), lambda h, i: (i, h)),
            out_shape=jax.ShapeDtypeStruct((t, qw), BF16),
            scratch_shapes=[pltpu.VMEM((group, HEAD_DIM, tq), F32)],
            compiler_params=_params(("parallel", "parallel"), 48),
            name="gqa_flash_attention_bounded" if bounded else "gqa_flash_attention_online",
        )(shift, q, k4, vt4)

    return lax.cond(score_bound < MAX_SAFE_SCORE_BOUND, lambda: call(True), lambda: call(False))


def _conv_kernel(a_ref, g_ref, ap_ref, gp_ref, an_ref, gn_ref, w_ref, b_ref, lg_ref, lb_ref,
                 o_ref, y_sc, c_sc, sh_sc, *, ksize, rows, lanes):
    i = pl.program_id(0)
    n = pl.num_programs(0)
    tt, cw = o_ref.shape

    def glu(a, g):
        a = a.astype(F32)
        g = g.astype(F32)
        return a * jax.nn.sigmoid(g)

    y_sc[HALO:HALO + tt, :] = glu(a_ref[...], g_ref[...])
    y_sc[0:HALO, :] = jnp.where(i > 0, glu(ap_ref[...], gp_ref[...]), 0.0)
    y_sc[HALO + tt:HALO + tt + HALO, :] = jnp.where(i < n - 1, glu(an_ref[...], gn_ref[...]), 0.0)

    base = HALO - ksize // 2
    span = sh_sc.shape[1]
    for c0 in range(0, cw, lanes):
        wc = w_ref[:, c0:c0 + lanes]
        bc = b_ref[:, c0:c0 + lanes]
        for s in range(SUBLANES):
            sh_sc[s] = y_sc[s:s + span, c0:c0 + lanes]
        for r0 in range(0, tt, rows):
            acc = jnp.broadcast_to(bc, (rows, lanes))
            for j in range(ksize):
                s = (base + j) % SUBLANES
                top = r0 + base + j - s
                acc = acc + wc[j:j + 1, :] * sh_sc[s, top:top + rows, :]
            c_sc[r0:r0 + rows, c0:c0 + lanes] = acc

    y = c_sc[...]
    mu = jnp.mean(y, axis=-1, keepdims=True)
    yc = y - mu
    var = jnp.mean(yc * yc, axis=-1, keepdims=True)
    z = yc * lax.rsqrt(var + EPS) * lg_ref[...] + lb_ref[...]
    o_ref[...] = (z * jax.nn.sigmoid(z)).astype(o_ref.dtype)


def conformer_conv_pre(u, dw_w, dw_b, ln_g, ln_b):
    t, c2 = u.shape
    cw = c2 // 2
    ksize = dw_w.shape[0]
    assert ksize // 2 <= HALO - 1
    tt = _tile(t, 256, HALO)
    hb = tt // HALO
    n_hb = t // HALO
    cur_a = pl.BlockSpec((tt, cw), lambda i: (i, 0))
    cur_g = pl.BlockSpec((tt, cw), lambda i: (i, 1))
    prev_a = pl.BlockSpec((HALO, cw), lambda i: (jnp.maximum(i * hb - 1, 0), 0))
    prev_g = pl.BlockSpec((HALO, cw), lambda i: (jnp.maximum(i * hb - 1, 0), 1))
    next_a = pl.BlockSpec((HALO, cw), lambda i: (jnp.minimum((i + 1) * hb, n_hb - 1), 0))
    next_g = pl.BlockSpec((HALO, cw), lambda i: (jnp.minimum((i + 1) * hb, n_hb - 1), 1))
    vec = pl.BlockSpec((1, cw), lambda i: (0, 0))
    lanes = min(512, cw)
    kern = functools.partial(_conv_kernel, ksize=ksize, rows=min(64, tt), lanes=lanes)
    return pl.pallas_call(
        kern,
        grid=(t // tt,),
        in_specs=[cur_a, cur_g, prev_a, prev_g, next_a, next_g,
                  pl.BlockSpec((ksize, cw), lambda i: (0, 0)), vec, vec, vec],
        out_specs=pl.BlockSpec((tt, cw), lambda i: (i, 0)),
        out_shape=jax.ShapeDtypeStruct((t, cw), BF16),
        scratch_shapes=[pltpu.VMEM((tt + 2 * HALO, cw), F32), pltpu.VMEM((tt, cw), F32),
                        pltpu.VMEM((SUBLANES, tt + 2 * HALO - SUBLANES, lanes), F32)],
        compiler_params=_params(("parallel",), 40),
        name="conformer_conv",
    )(u, u, u, u, u, u, dw_w, dw_b, ln_g, ln_b)


def _merge_kernel(a_ref, wa_ref, b_ref, wb_ref, ga_ref, gb_ref, o_ref):
    pa = jnp.dot(a_ref[...], wa_ref[...], preferred_element_type=F32)
    pb = jnp.dot(b_ref[...], wb_ref[...], preferred_element_type=F32)
    ga = jax.nn.sigmoid(ga_ref[...].astype(F32))
    gb = jax.nn.sigmoid(gb_ref[...].astype(F32))
    o_ref[...] = (ga * pa + gb * pb).astype(o_ref.dtype)


def gated_merge(attn, w_attn_o, conv, w_conv_o, gates):
    t, ka = attn.shape
    kb = conv.shape[1]
    d = w_attn_o.shape[1]
    tm = _tile(t, 512, 8)
    tn = _tile(d, 1024)
    nb = d // tn
    return pl.pallas_call(
        _merge_kernel,
        grid=(nb, t // tm),
        in_specs=[
            pl.BlockSpec((tm, ka), lambda j, i: (i, 0)),
            pl.BlockSpec((ka, tn), lambda j, i: (0, j)),
            pl.BlockSpec((tm, kb), lambda j, i: (i, 0)),
            pl.BlockSpec((kb, tn), lambda j, i: (0, j)),
            pl.BlockSpec((tm, tn), lambda j, i: (i, j)),
            pl.BlockSpec((tm, tn), lambda j, i: (i, nb + j)),
        ],
        out_specs=pl.BlockSpec((tm, tn), lambda j, i: (i, j)),
        out_shape=jax.ShapeDtypeStruct((t, d), BF16),
        compiler_params=_params(("parallel", "parallel"), 48),
        name="gated_merge",
    )(attn, w_attn_o, conv, w_conv_o, gates, gates)


def _mm_resid_kernel(a_ref, w_ref, x_ref, g_ref, o_ref):
    acc = jnp.dot(a_ref[...], w_ref[...], preferred_element_type=F32)
    o_ref[...] = x_ref[...] + g_ref[...] * acc


def matmul_gated_residual(a, w, x, gate):
    m, k = a.shape
    n = w.shape[1]
    tm = _tile(m, 1024, 8)
    tn = _tile(n, 1024)
    return pl.pallas_call(
        _mm_resid_kernel,
        grid=(n // tn, m // tm),
        in_specs=[
            pl.BlockSpec((tm, k), lambda j, i: (i, 0)),
            pl.BlockSpec((k, tn), lambda j, i: (0, j)),
            pl.BlockSpec((tm, tn), lambda j, i: (i, j)),
            pl.BlockSpec((1, tn), lambda j, i: (0, j)),
        ],
        out_specs=pl.BlockSpec((tm, tn), lambda j, i: (i, j)),
        out_shape=jax.ShapeDtypeStruct((m, n), F32),
        compiler_params=_params(("parallel", "parallel"), 56),
        name="out_proj_residual",
    )(a, w, x, gate)


HI_HALF_MASK = -65536


def _pack_bf16_pair(lo, hi):
    lo_b = lax.bitcast_convert_type(lo.astype(BF16).astype(F32), jnp.int32)
    hi_b = lax.bitcast_convert_type(hi.astype(BF16).astype(F32), jnp.int32)
    return (hi_b & HI_HALF_MASK) | lax.shift_right_logical(lo_b, jnp.full(lo_b.shape, 16, jnp.int32))


def _unpack_bf16_pair(w):
    lo = lax.bitcast_convert_type(w << 16, F32)
    hi = lax.bitcast_convert_type(w & HI_HALF_MASK, F32)
    return lo, hi


def _router_kernel(x_ref, g_ref, sc_ref, sh_ref, wh_ref, wl_ref, h_ref, aff_ref, *, n_experts):
    x = x_ref[...]
    half = x.shape[1] // 2
    ms = jnp.mean(x * x, axis=-1, keepdims=True)
    h = x * lax.rsqrt(ms + EPS) * g_ref[...]
    h = h * (1.0 + sc_ref[...]) + sh_ref[...]
    hb = h.astype(BF16)
    h_ref[...] = _pack_bf16_pair(h[:, :half], h[:, half:])
    hl = (h - hb.astype(F32)).astype(BF16)
    logits = (jnp.dot(hb, wh_ref[...], preferred_element_type=F32)
              + jnp.dot(hb, wl_ref[...], preferred_element_type=F32)
              + jnp.dot(hl, wh_ref[...], preferred_element_type=F32))
    lane = lax.broadcasted_iota(jnp.int32, logits.shape, 1)
    logits = jnp.where(lane < n_experts, logits, -jnp.inf)
    mx = jnp.max(logits, axis=-1, keepdims=True)
    e = jnp.exp(logits - mx)
    aff_ref[...] = e / jnp.sum(e, axis=-1, keepdims=True)


def norm_router(x, g, scale, shift, w_router):
    t, d = x.shape
    n_experts = w_router.shape[1]
    wp = jnp.zeros((d, LANE), F32).at[:, :n_experts].set(w_router)
    wh = wp.astype(BF16)
    wl = (wp - wh.astype(F32)).astype(BF16)
    tm = _tile(t, 256, 8)
    vec = pl.BlockSpec((1, d), lambda i: (0, 0))
    wspec = pl.BlockSpec((d, LANE), lambda i: (0, 0))
    kern = functools.partial(_router_kernel, n_experts=n_experts)
    return pl.pallas_call(
        kern,
        grid=(t // tm,),
        in_specs=[pl.BlockSpec((tm, d), lambda i: (i, 0)), vec, vec, vec, wspec, wspec],
        out_specs=[pl.BlockSpec((tm, d // 2), lambda i: (i, 0)), pl.BlockSpec((tm, LANE), lambda i: (i, 0))],
        out_shape=[jax.ShapeDtypeStruct((t, d // 2), jnp.int32), jax.ShapeDtypeStruct((t, LANE), F32)],
        compiler_params=_params(("parallel",), 40),
        name="norm_router",
    )(x, g, scale, shift, wh, wl)


def _block_cumsum(m):
    nb = m.shape[0]
    r = lax.broadcasted_iota(jnp.int32, (LANE, LANE), 0)
    c = lax.broadcasted_iota(jnp.int32, (LANE, LANE), 1)
    upper = jnp.where(r <= c, 1.0, 0.0).astype(BF16)
    inblock = jnp.dot(m.astype(BF16), upper, preferred_element_type=F32)
    tot = jnp.broadcast_to(inblock[:, LANE - 1:LANE], (nb, LANE))
    rb = lax.broadcasted_iota(jnp.int32, (nb, nb), 0)
    cb = lax.broadcasted_iota(jnp.int32, (nb, nb), 1)
    lower = jnp.where(cb < rb, 1.0, 0.0).astype(BF16)
    before = jnp.dot(lower, tot.astype(BF16), preferred_element_type=F32)
    return inblock, before


def _select_kernel(a_ref, idx_ref, val_ref, soff_ref, *, cap):
    a = a_ref[0]
    nb = a.shape[0]
    bits = lax.bitcast_convert_type(a, jnp.int32)
    capf = jnp.float32(cap)

    def count(pred):
        cnt = jnp.sum(jnp.where(pred, 1.0, 0.0), axis=0, keepdims=True)
        return jnp.sum(cnt, axis=1, keepdims=True)

    def bisect(i, thr):
        cand = thr | lax.shift_left(jnp.int32(1), 30 - i)
        return jnp.where(count(bits >= cand) >= capf, cand, thr)

    thr = lax.fori_loop(0, 31, bisect, jnp.zeros((1, 1), jnp.int32))
    gt = bits > thr
    eq = bits == thr
    need = capf - count(gt)
    eqf = jnp.where(eq, 1.0, 0.0)
    eq_in, eq_before = _block_cumsum(eqf)
    take_eq = eq & ((eq_in + eq_before - eqf) < need)
    sel = jnp.where(gt | take_eq, 1.0, 0.0)
    inblock, before = _block_cumsum(sel)

    slot = lax.broadcasted_iota(jnp.int32, (1, cap), 1).astype(F32)
    upto = before[:, 0:1] + inblock[:, LANE - 1:LANE]
    blk = jnp.sum(jnp.where(upto <= slot, 1.0, 0.0), axis=0, keepdims=True)
    bi = lax.broadcasted_iota(jnp.int32, (nb, cap), 0).astype(F32)
    onehot = jnp.where(bi == blk, 1.0, 0.0).astype(BF16)

    rows = jnp.dot(inblock.T.astype(BF16), onehot, preferred_element_type=F32)
    before_t = before.T
    hi = jnp.floor(before_t * (1.0 / LANE))
    lo = before_t - hi * LANE
    start = (jnp.dot(hi.astype(BF16), onehot, preferred_element_type=F32) * LANE
             + jnp.dot(lo.astype(BF16), onehot, preferred_element_type=F32))
    rank = slot - start
    lane_of = jnp.sum(jnp.where(rows <= rank, 1.0, 0.0), axis=0, keepdims=True)
    idx_ref[0] = (blk * LANE + lane_of).astype(jnp.int32)

    at = a.T
    a1 = at.astype(BF16)
    r1 = at - a1.astype(F32)
    a2 = r1.astype(BF16)
    a3 = (r1 - a2.astype(F32)).astype(BF16)
    picked = (jnp.dot(a1, onehot, preferred_element_type=F32)
              + jnp.dot(a2, onehot, preferred_element_type=F32)
              + jnp.dot(a3, onehot, preferred_element_type=F32))
    li = lax.broadcasted_iota(jnp.int32, (LANE, cap), 0).astype(F32)
    val_ref[0] = jnp.sum(jnp.where(li == lane_of, picked, 0.0), axis=0, keepdims=True)
    soff_ref[0] = before_t[0:1, :].astype(jnp.int32)


def expert_select(aff3, cap):
    e, nb, _ = aff3.shape
    kern = functools.partial(_select_kernel, cap=cap)
    return pl.pallas_call(
        kern,
        grid=(e,),
        in_specs=[pl.BlockSpec((1, nb, LANE), lambda i: (i, 0, 0))],
        out_specs=[pl.BlockSpec((1, 1, cap), lambda i: (i, 0, 0)),
                   pl.BlockSpec((1, 1, cap), lambda i: (i, 0, 0)),
                   pl.BlockSpec((1, 1, nb), lambda i: (i, 0, 0))],
        out_shape=[jax.ShapeDtypeStruct((e, 1, cap), jnp.int32),
                   jax.ShapeDtypeStruct((e, 1, cap), F32),
                   jax.ShapeDtypeStruct((e, 1, nb), jnp.int32)],
        compiler_params=_params(("parallel",), 32),
        name="expert_select",
    )(aff3)


def _expert_up_kernel(idx_ref, h_hbm, wg_ref, wu_ref, o_ref, stage, x_sc, sem, *, cap, phase_rows):
    e = pl.program_id(0)
    j = pl.program_id(1)
    half = x_sc.shape[1] // 2
    n_phases = cap // phase_rows
    unroll = SUBLANES
    assert phase_rows % unroll == 0

    def row_copy(p, r):
        tok = idx_ref[e * cap + p * phase_rows + r]
        return pltpu.make_async_copy(h_hbm.at[pl.ds(tok, 1), :],
                                     stage.at[p % 2, pl.ds(r, 1), :], sem.at[p % 2])

    def start_phase(p):
        def go(r, carry):
            for k in range(unroll):
                row_copy(p, r * unroll + k).start()
            return carry
        lax.fori_loop(0, phase_rows // unroll, go, 0)

    def wait_phase(p):
        def go(r, carry):
            for k in range(unroll):
                row_copy(p, r * unroll + k).wait()
            return carry
        lax.fori_loop(0, phase_rows // unroll, go, 0)

    @pl.when(j == 0)
    def _():
        start_phase(0)
        for p in range(n_phases):
            if p + 1 < n_phases:
                start_phase(p + 1)
            wait_phase(p)
            lo, hi = _unpack_bf16_pair(stage[p % 2])
            x_sc[p * phase_rows:(p + 1) * phase_rows, :half] = lo.astype(BF16)
            x_sc[p * phase_rows:(p + 1) * phase_rows, half:] = hi.astype(BF16)

    tf = o_ref.shape[2]
    w = jnp.concatenate([wg_ref[0].astype(BF16), wu_ref[0].astype(BF16)], axis=1)
    gu = jnp.dot(x_sc[...], w, preferred_element_type=F32)
    g, u = gu[:, :tf], gu[:, tf:]
    o_ref[0] = (g * jax.nn.sigmoid(g) * u).astype(o_ref.dtype)


def expert_up(idx_flat, h_packed, w_gate, w_up, cap):
    e, d, f = w_gate.shape
    half = d // 2
    tf = _tile(f, 256)
    phase_rows = _tile(cap, 512, 8)
    wspec = pl.BlockSpec((1, d, tf), lambda i, j, idx: (i, 0, j))
    kern = functools.partial(_expert_up_kernel, cap=cap, phase_rows=phase_rows)
    return pl.pallas_call(
        kern,
        grid_spec=pltpu.PrefetchScalarGridSpec(
            num_scalar_prefetch=1,
            grid=(e, f // tf),
            in_specs=[pl.BlockSpec(memory_space=pl.ANY), wspec, wspec],
            out_specs=pl.BlockSpec((1, cap, tf), lambda i, j, idx: (i, 0, j)),
            scratch_shapes=[pltpu.VMEM((2, phase_rows, half), jnp.int32),
                            pltpu.VMEM((cap, d), BF16),
                            pltpu.SemaphoreType.DMA((2,))],
        ),
        out_shape=jax.ShapeDtypeStruct((e, cap, f), BF16),
        compiler_params=_params(("arbitrary", "arbitrary"), 56),
        name="expert_up_gather",
    )(idx_flat, h_packed, w_gate, w_up)


def _expert_down_kernel(h_ref, wa_ref, wb_ref, v_ref, o_ref):
    h = h_ref[0]
    v = v_ref[0]
    tn = o_ref.shape[2]
    w = jnp.concatenate([wa_ref[0].astype(BF16), wb_ref[0].astype(BF16)], axis=1)
    y = jnp.dot(h, w, preferred_element_type=F32) * v
    o_ref[0] = _pack_bf16_pair(y[:, :tn], y[:, tn:])


def expert_down(hid, w_down, vals):
    e, c, f = hid.shape
    d = w_down.shape[2]
    half = d // 2
    tn = _tile(half, 256)
    nb = half // tn
    return pl.pallas_call(
        _expert_down_kernel,
        grid=(e, nb),
        in_specs=[pl.BlockSpec((1, c, f), lambda i, j: (i, 0, 0)),
                  pl.BlockSpec((1, f, tn), lambda i, j: (i, 0, j)),
                  pl.BlockSpec((1, f, tn), lambda i, j: (i, 0, nb + j)),
                  pl.BlockSpec((1, c, 1), lambda i, j: (i, 0, 0))],
        out_specs=pl.BlockSpec((1, c, tn), lambda i, j: (i, 0, j)),
        out_shape=jax.ShapeDtypeStruct((e, c, half), jnp.int32),
        compiler_params=_params(("parallel", "arbitrary"), 48),
        name="expert_down",
    )(hid, w_down, w_down, vals)


def _combine_kernel(idx_ref, soff_ref, x_ref, y_hbm, g2_ref, fg_ref, o_ref, stage, sem,
                    *, cap, n_blocks, n_experts):
    b = pl.program_id(0)
    tt, d = x_ref.shape
    half = d // 2
    n_groups = sem.shape[0]
    per_group = n_experts // n_groups

    def slot_range(e):
        lo = soff_ref[e * n_blocks + b]
        nxt = soff_ref[e * n_blocks + jnp.minimum(b + 1, n_blocks - 1)]
        return lo, jnp.where(b + 1 < n_blocks, nxt, cap)

    def row_copy(e, j):
        tok = idx_ref[e * cap + j]
        return pltpu.make_async_copy(y_hbm.at[e, pl.ds(j, 1), :],
                                     stage.at[e, pl.ds(tok - b * tt, 1), :], sem.at[e // per_group])

    for grp in range(n_groups):
        first = grp * per_group
        stage[first:first + per_group] = jnp.zeros((per_group,) + stage.shape[1:], stage.dtype)
        for e in range(first, first + per_group):
            lo, hi = slot_range(e)

            def go(j, carry, e=e):
                row_copy(e, j).start()
                return carry
            lax.fori_loop(lo, hi, go, 0)

    for e in range(n_experts):
        lo, hi = slot_range(e)

        def done(j, carry, e=e):
            row_copy(e, j).wait()
            return carry
        lax.fori_loop(lo, hi, done, 0)

    f_lo = jnp.zeros((tt, half), F32)
    f_hi = jnp.zeros((tt, half), F32)
    for e in range(n_experts):
        lo, hi = _unpack_bf16_pair(stage[e])
        f_lo = f_lo + lo
        f_hi = f_hi + hi
    x = x_ref[...]
    x_lo = x[:, :half] + g2_ref[:, :half] * f_lo
    x_hi = x[:, half:] + g2_ref[:, half:] * f_hi
    ms = (jnp.sum(x_lo * x_lo, axis=-1, keepdims=True)
          + jnp.sum(x_hi * x_hi, axis=-1, keepdims=True)) * (1.0 / d)
    inv = lax.rsqrt(ms + EPS)
    o_ref[:, :half] = x_lo * inv * fg_ref[:, :half]
    o_ref[:, half:] = x_hi * inv * fg_ref[:, half:]


COMBINE_GROUPS = 4


def combine_final_norm(idx_flat, soff_flat, x, y_packed, gate, final_g, cap):
    t, d = x.shape
    n_experts = y_packed.shape[0]
    tt = LANE
    n_blocks = t // tt
    vec = pl.BlockSpec((1, d), lambda i, a, b: (0, 0))
    row = pl.BlockSpec((tt, d), lambda i, a, b: (i, 0))
    kern = functools.partial(_combine_kernel, cap=cap, n_blocks=n_blocks, n_experts=n_experts)
    return pl.pallas_call(
        kern,
        grid_spec=pltpu.PrefetchScalarGridSpec(
            num_scalar_prefetch=2,
            grid=(n_blocks,),
            in_specs=[row, pl.BlockSpec(memory_space=pl.ANY), vec, vec],
            out_specs=row,
            scratch_shapes=[pltpu.VMEM((n_experts, tt, d // 2), jnp.int32),
                            pltpu.SemaphoreType.DMA((COMBINE_GROUPS if n_experts % COMBINE_GROUPS == 0 else 1,))],
        ),
        out_shape=jax.ShapeDtypeStruct((t, d), F32),
        compiler_params=_params(("arbitrary",), 48),
        name="combine_final_norm",
    )(idx_flat, soff_flat, x, y_packed, gate, final_g)


def kernel(x, c, ctx, c_ctx, w_ada, b_ada, norm1_g, w_in, b_in, q_norm_g, k_norm_g, w_attn_o,
           dw_w, dw_b, conv_ln_g, conv_ln_b, w_conv_o, w_out, norm2_g, w_router, w_gate, w_up,
           w_down, final_g):
    batch, t, d = x.shape
    depth = w_ada.shape[0]
    assert batch == 1 and depth == 1
    q_w = w_attn_o.shape[1]
    conv_w = w_conv_o.shape[1]
    in_w = w_in.shape[2]
    kv_w = (in_w - q_w - 2 * conv_w - 2 * d) // 2
    n_kv = kv_w // HEAD_DIM
    n_experts = w_router.shape[2]
    cap = CAPACITY_FACTOR * t // n_experts
    o1, o2, o3, o4 = q_w, q_w + kv_w, q_w + 2 * kv_w, q_w + 2 * kv_w + 2 * conv_w

    row = lambda v: v.reshape(1, -1)
    x2, ctx2 = x[0], ctx[0]

    cvecs_t = jnp.stack([c[0], c_ctx], axis=1)
    mod = ada_modulation(cvecs_t, w_ada[0], row(b_ada[0]))
    sh1, sc1, g1, sh2, sc2, g2 = [mod[0:1, i * d:(i + 1) * d] for i in range(N_MOD)]
    csh1, csc1 = mod[1:2, 0:d], mod[1:2, d:2 * d]

    h_lat = norm_modulate(x2, row(norm1_g[0]), sc1, sh1)
    h_ctx = norm_modulate(ctx2, row(norm1_g[0]), csc1, csh1)

    w_in0, b_in0 = w_in[0], row(b_in[0])
    cosf, sins = rope_tables(t)
    qg, kg = row(q_norm_g[0]), row(k_norm_g[0])
    gw = q_w // n_kv
    assert gw == kv_w
    q_scale = HEAD_DIM ** -0.5 * LOG2E
    qkv = qkv_projection(h_lat, w_in0, b_in0, qg, kg, cosf, sins,
                         block_w=gw, n_q_blocks=n_kv, q_scale=q_scale)
    u = matmul_bias(h_lat, w_in0, b_in0, o3, o4 - o3)
    gates = matmul_bias(h_lat, w_in0, b_in0, o4, in_w - o4)
    kv_ctx = matmul_bias(h_ctx, w_in0, b_in0, o1, o3 - o1)

    tc = ctx2.shape[0]
    k_ctx = head_norm(kv_ctx, 0, 1, kv_w, kg, cosf[:tc], sins[:tc], rope=False, scale=1.0)
    k_all = jnp.concatenate([k_ctx, qkv[:, o1:o2]], axis=0)
    v_all = jnp.concatenate([kv_ctx[:, kv_w:], qkv[:, o2:o3]], axis=0)
    score_bound = (1.02 * HEAD_DIM * q_scale) * jnp.max(jnp.abs(q_norm_g[0])) * jnp.max(jnp.abs(k_norm_g[0]))
    attn = gqa_attention(qkv, q_w, k_all, v_all, n_kv, score_bound)

    conv = conformer_conv_pre(u, dw_w[0], row(dw_b[0]), row(conv_ln_g[0]), row(conv_ln_b[0]))
    merged = gated_merge(attn, w_attn_o[0].astype(BF16), conv, w_conv_o[0].astype(BF16), gates)
    x1 = matmul_gated_residual(merged, w_out[0].astype(BF16), x2, g1)

    h2p, aff = norm_router(x1, row(norm2_g[0]), sc2, sh2, w_router[0])
    aff3 = aff[:, :n_experts].T.reshape(n_experts, t // LANE, LANE)
    idx, vals, soff = expert_select(aff3, cap)
    idx_flat = idx.reshape(-1)
    hid = expert_up(idx_flat, h2p, w_gate[0], w_up[0], cap)
    y = expert_down(hid, w_down[0], vals.reshape(n_experts, cap, 1))
    out = combine_final_norm(idx_flat, soff.reshape(-1), x1, y, g2, row(final_g), cap)
    return out[None]
```

```python
import functools

import jax
import jax.numpy as jnp
from jax import lax
from jax.experimental import pallas as pl
from jax.experimental.pallas import tpu as pltpu

F32 = jnp.float32
BF16 = jnp.bfloat16

HEAD_DIM = 128
GRID_W = 64
ROPE_THETA = 10000.0
ROPE_AXIS_DIM = HEAD_DIM // 2
EPS = 1e-6
CAPACITY_FACTOR = 2
N_MOD = 6
HALO = 16
LANE = 128
SUBLANES = 8
MIB = 1024 * 1024
LOG2E = 1.4426950408889634


def _tile(n, pref, mult=LANE):
    if n <= pref:
        return n
    t = (pref // mult) * mult
    while t >= mult:
        if n % t == 0:
            return t
        t -= mult
    return n


def _params(sem, vmem_mib):
    return pltpu.CompilerParams(dimension_semantics=sem, vmem_limit_bytes=vmem_mib * MIB)


def _ada_kernel(s_ref, w_ref, b_ref, o_ref):
    k = pl.program_id(1)

    @pl.when(k == 0)
    def _():
        o_ref[...] = jnp.broadcast_to(b_ref[...], o_ref.shape)

    s = s_ref[...]
    s = s * jax.nn.sigmoid(s)
    w = w_ref[...]
    r0 = jnp.sum(s[:, 0:1] * w, axis=0, keepdims=True)
    r1 = jnp.sum(s[:, 1:2] * w, axis=0, keepdims=True)
    o_ref[0:1, :] += r0
    o_ref[1:2, :] += r1


def ada_modulation(cvecs_t, w_ada, b_ada):
    d, n = w_ada.shape
    tk = _tile(d, 1024, 8)
    tn = _tile(n, 2048)
    return pl.pallas_call(
        _ada_kernel,
        grid=(n // tn, d // tk),
        in_specs=[
            pl.BlockSpec((tk, 2), lambda j, k: (k, 0)),
            pl.BlockSpec((tk, tn), lambda j, k: (k, j)),
            pl.BlockSpec((1, tn), lambda j, k: (0, j)),
        ],
        out_specs=pl.BlockSpec((2, tn), lambda j, k: (0, j)),
        out_shape=jax.ShapeDtypeStruct((2, n), F32),
        compiler_params=_params(("parallel", "arbitrary"), 48),
        name="ada_modulation",
    )(cvecs_t, w_ada, b_ada)


def _normmod_kernel(x_ref, g_ref, sc_ref, sh_ref, o_ref):
    x = x_ref[...].astype(F32)
    ms = jnp.mean(x * x, axis=-1, keepdims=True)
    y = x * lax.rsqrt(ms + EPS) * g_ref[...]
    o_ref[...] = (y * (1.0 + sc_ref[...]) + sh_ref[...]).astype(o_ref.dtype)


def norm_modulate(x, g, scale, shift, out_dtype=BF16):
    t, d = x.shape
    tm = _tile(t, 256, 8)
    vec = pl.BlockSpec((1, d), lambda i: (0, 0))
    return pl.pallas_call(
        _normmod_kernel,
        grid=(t // tm,),
        in_specs=[pl.BlockSpec((tm, d), lambda i: (i, 0)), vec, vec, vec],
        out_specs=pl.BlockSpec((tm, d), lambda i: (i, 0)),
        out_shape=jax.ShapeDtypeStruct((t, d), out_dtype),
        compiler_params=_params(("parallel",), 40),
        name="norm_modulate",
    )(x, g, scale, shift)


W_COLS = 512


def _mm_bias_kernel(a_ref, w_ref, b_ref, o_ref, w_sc):
    @pl.when(pl.program_id(1) == 0)
    def _():
        w_sc[...] = w_ref[...].astype(BF16)

    acc = jnp.dot(a_ref[...], w_sc[...], preferred_element_type=F32)
    o_ref[...] = (acc + b_ref[...]).astype(o_ref.dtype)


def matmul_bias(a, w, b, col0, n, out_dtype=BF16):
    m, k = a.shape
    tm = _tile(m, 1024, 8)
    tn = _tile(n, W_COLS)
    assert col0 % tn == 0
    c0 = col0 // tn
    return pl.pallas_call(
        _mm_bias_kernel,
        grid=(n // tn, m // tm),
        in_specs=[
            pl.BlockSpec((tm, k), lambda j, i: (i, 0)),
            pl.BlockSpec((k, tn), lambda j, i: (0, c0 + j)),
            pl.BlockSpec((1, tn), lambda j, i: (0, c0 + j)),
        ],
        out_specs=pl.BlockSpec((tm, tn), lambda j, i: (i, j)),
        out_shape=jax.ShapeDtypeStruct((m, n), out_dtype),
        scratch_shapes=[pltpu.VMEM((k, tn), BF16)],
        compiler_params=_params(("parallel", "arbitrary"), 56),
        name="matmul_bias",
    )(a, w, b)


def _norm_heads(x, g, o_ref, *, scale, cosf=None, sins=None):
    if cosf is not None:
        lane = lax.broadcasted_iota(jnp.int32, cosf.shape, 1)
        first_half = (lane % (ROPE_AXIS_DIM)) < (ROPE_AXIS_DIM // 2)
    for h in range(x.shape[1] // HEAD_DIM):
        xh = x[:, h * HEAD_DIM:(h + 1) * HEAD_DIM]
        ms = jnp.mean(xh * xh, axis=-1, keepdims=True)
        y = xh * lax.rsqrt(ms + EPS) * g
        if cosf is not None:
            up = pltpu.roll(y, HEAD_DIM - ROPE_AXIS_DIM // 2, axis=1)
            dn = pltpu.roll(y, ROPE_AXIS_DIM // 2, axis=1)
            y = y * cosf + jnp.where(first_half, up, dn) * sins
        if scale != 1.0:
            y = y * scale
        o_ref[:, h * HEAD_DIM:(h + 1) * HEAD_DIM] = y.astype(o_ref.dtype)


def _headnorm_kernel(x_ref, g_ref, cos_ref, sin_ref, o_ref, *, n_heads, rope, scale):
    x = x_ref[...].astype(F32)
    if rope:
        _norm_heads(x, g_ref[...], o_ref, scale=scale, cosf=cos_ref[...], sins=sin_ref[...])
    else:
        _norm_heads(x, g_ref[...], o_ref, scale=scale)


def _qkv_kernel(a_ref, w_ref, b_ref, qg_ref, kg_ref, cos_ref, sin_ref, o_ref, w_sc,
                *, n_q_blocks, q_scale, row_chunk):
    j = pl.program_id(0)

    @pl.when(pl.program_id(1) == 0)
    def _():
        w_sc[...] = w_ref[...].astype(BF16)

    is_q = j < n_q_blocks
    is_v = j > n_q_blocks
    g = jnp.where(is_q, qg_ref[...], kg_ref[...])
    scale = jnp.where(is_q, jnp.float32(q_scale), jnp.float32(1.0))
    w = w_sc[...]
    bias = b_ref[...]
    lane = lax.broadcasted_iota(jnp.int32, (row_chunk, HEAD_DIM), 1)
    first_half = (lane % (ROPE_AXIS_DIM)) < (ROPE_AXIS_DIM // 2)
    for r0 in range(0, a_ref.shape[0], row_chunk):
        rows = slice(r0, r0 + row_chunk)
        acc = jnp.dot(a_ref[rows, :], w, preferred_element_type=F32) + bias
        cosf, sins = cos_ref[rows, :], sin_ref[rows, :]
        for h in range(w.shape[1] // HEAD_DIM):
            cols = slice(h * HEAD_DIM, (h + 1) * HEAD_DIM)
            xh = acc[:, cols]
            ms = jnp.mean(xh * xh, axis=-1, keepdims=True)
            y = xh * lax.rsqrt(ms + EPS) * g
            up = pltpu.roll(y, HEAD_DIM - ROPE_AXIS_DIM // 2, axis=1)
            dn = pltpu.roll(y, ROPE_AXIS_DIM // 2, axis=1)
            y = (y * cosf + jnp.where(first_half, up, dn) * sins) * scale
            o_ref[rows, cols] = jnp.where(is_v, xh, y).astype(o_ref.dtype)


def qkv_projection(a, w, b, qg, kg, cos, sin, *, block_w, n_q_blocks, q_scale):
    m, k = a.shape
    n = (n_q_blocks + 2) * block_w
    tm = _tile(m, 1024, 8)
    kern = functools.partial(_qkv_kernel, n_q_blocks=n_q_blocks, q_scale=q_scale,
                             row_chunk=_tile(tm, 256, 8))
    gain = pl.BlockSpec((1, HEAD_DIM), lambda j, i: (0, 0))
    tab = pl.BlockSpec((tm, HEAD_DIM), lambda j, i: (i, 0))
    return pl.pallas_call(
        kern,
        grid=(n // block_w, m // tm),
        in_specs=[
            pl.BlockSpec((tm, k), lambda j, i: (i, 0)),
            pl.BlockSpec((k, block_w), lambda j, i: (0, j)),
            pl.BlockSpec((1, block_w), lambda j, i: (0, j)),
            gain, gain, tab, tab,
        ],
        out_specs=pl.BlockSpec((tm, block_w), lambda j, i: (i, j)),
        out_shape=jax.ShapeDtypeStruct((m, n), BF16),
        scratch_shapes=[pltpu.VMEM((k, block_w), BF16)],
        compiler_params=_params(("parallel", "arbitrary"), 56),
        name="qkv_projection",
    )(a, w, b, qg, kg, cos, sin)


def head_norm(x, col_block, n_blocks, width, g, cos, sin, *, rope, scale):
    t = x.shape[0]
    tm = _tile(t, 512, 8)
    n_heads = width // HEAD_DIM
    kern = functools.partial(_headnorm_kernel, n_heads=n_heads, rope=rope, scale=scale)
    tab = pl.BlockSpec((tm, HEAD_DIM), lambda i, j: (i, 0))
    return pl.pallas_call(
        kern,
        grid=(t // tm, n_blocks),
        in_specs=[
            pl.BlockSpec((tm, width), lambda i, j: (i, col_block + j)),
            pl.BlockSpec((1, HEAD_DIM), lambda i, j: (0, 0)),
            tab, tab,
        ],
        out_specs=pl.BlockSpec((tm, width), lambda i, j: (i, j)),
        out_shape=jax.ShapeDtypeStruct((t, n_blocks * width), BF16),
        compiler_params=_params(("parallel", "parallel"), 32),
        name="head_norm_rope",
    )(x, g, cos, sin)


def rope_tables(n_tokens):
    n_rows = n_tokens // GRID_W
    row = jnp.repeat(jnp.arange(n_rows, dtype=jnp.int32), GRID_W)
    col = jnp.tile(jnp.arange(GRID_W, dtype=jnp.int32), n_rows)
    pos = jnp.stack([row, col], axis=-1).astype(F32)
    inv_freq = ROPE_THETA ** (-jnp.arange(0, ROPE_AXIS_DIM, 2, dtype=F32) / ROPE_AXIS_DIM)
    ang = pos[:, :, None] * inv_freq
    cos, sin = jnp.cos(ang), jnp.sin(ang)
    cosf = jnp.concatenate([cos, cos], axis=-1).reshape(n_tokens, HEAD_DIM)
    sins = jnp.concatenate([-sin, sin], axis=-1).reshape(n_tokens, HEAD_DIM)
    return cosf, sins


def _attn_kernel(b_ref, q_ref, k_ref, vt_ref, o_ref, acc_sc, *, group, n_chunks, bounded):
    tq = q_ref.shape[0]
    acc_sc[...] = jnp.zeros_like(acc_sc)
    qs = [q_ref[:, g * HEAD_DIM:(g + 1) * HEAD_DIM] for g in range(group)]
    shift = b_ref[0, 0]

    def body(c, carry):
        ms, ls = carry
        kc = k_ref[0, c]
        vc = vt_ref[0, c]
        new_m, new_l = [], []
        for g in range(group):
            s = lax.dot_general(kc, qs[g], (((1,), (1,)), ((), ())),
                                preferred_element_type=F32)
            if bounded:
                p = jnp.exp2(s - shift)
                new_m.append(ms[g])
                new_l.append(ls[g] + jnp.sum(p, axis=0, keepdims=True))
                acc_sc[g] += jnp.dot(vc, p.astype(BF16), preferred_element_type=F32)
            else:
                m_new = jnp.maximum(ms[g], jnp.max(s, axis=0, keepdims=True))
                p = jnp.exp2(s - m_new)
                alpha = jnp.exp2(ms[g] - m_new)
                new_m.append(m_new)
                new_l.append(alpha * ls[g] + jnp.sum(p, axis=0, keepdims=True))
                pv = jnp.dot(vc, p.astype(BF16), preferred_element_type=F32)
                acc_sc[g] = alpha * acc_sc[g] + pv
        return tuple(new_m), tuple(new_l)

    m0 = tuple(jnp.full((1, tq), -jnp.inf, F32) for _ in range(group))
    l0 = tuple(jnp.zeros((1, tq), F32) for _ in range(group))
    _, ls = lax.fori_loop(0, n_chunks, body, (m0, l0))
    for g in range(group):
        out_t = acc_sc[g] / ls[g]
        o_ref[:, g * HEAD_DIM:(g + 1) * HEAD_DIM] = out_t.T.astype(o_ref.dtype)


MAX_SAFE_SCORE_BOUND = 60.0


def gqa_attention(q, qw, k_all, v_all, n_kv_heads, score_bound):
    t = q.shape[0]
    tk_all = k_all.shape[0]
    group = qw // HEAD_DIM // n_kv_heads
    gw = group * HEAD_DIM
    tq = _tile(t, 512)
    tkc = _tile(tk_all, 1280)
    nch = tk_all // tkc
    k4 = k_all.reshape(nch, tkc, n_kv_heads, HEAD_DIM).transpose(2, 0, 1, 3)
    vt4 = v_all.reshape(nch, tkc, n_kv_heads, HEAD_DIM).transpose(2, 0, 3, 1)
    shift = score_bound.reshape(1, 1).astype(F32)

    def call(bounded):
        kern = functools.partial(_attn_kernel, group=group, n_chunks=nch, bounded=bounded)
        return pl.pallas_call(
            kern,
            grid=(n_kv_heads, t // tq),
            in_specs=[
                pl.BlockSpec(memory_space=pltpu.SMEM),
                pl.BlockSpec((tq, gw), lambda h, i: (i, h)),
                pl.BlockSpec((1, nch, tkc, HEAD_DIM), lambda h, i: (h, 0, 0, 0)),
                pl.BlockSpec((1, nch, HEAD_DIM, tkc), lambda h, i: (h, 0, 0, 0)),
            ],
            out_specs=pl.BlockSpec((tq, gw), lambda h, i: (i, h)),
            out_shape=jax.ShapeDtypeStruct((t, qw), BF16),
            scratch_shapes=[pltpu.VMEM((group, HEAD_DIM, tq), F32)],
            compiler_params=_params(("parallel", "parallel"), 48),
            name="gqa_flash_attention_bounded" if bounded else "gqa_flash_attention_online",
        )(shift, q, k4, vt4)

    return lax.cond(score_bound < MAX_SAFE_SCORE_BOUND, lambda: call(True), lambda: call(False))


def _conv_kernel(a_ref, g_ref, ap_ref, gp_ref, an_ref, gn_ref, w_ref, b_ref, lg_ref, lb_ref,
                 o_ref, y_sc, c_sc, sh_sc, *, ksize, rows, lanes):
    i = pl.program_id(0)
    n = pl.num_programs(0)
    tt, cw = o_ref.shape

    def glu(a, g):
        a = a.astype(F32)
        g = g.astype(F32)
        return a * jax.nn.sigmoid(g)

    y_sc[HALO:HALO + tt, :] = glu(a_ref[...], g_ref[...])
    y_sc[0:HALO, :] = jnp.where(i > 0, glu(ap_ref[...], gp_ref[...]), 0.0)
    y_sc[HALO + tt:HALO + tt + HALO, :] = jnp.where(i < n - 1, glu(an_ref[...], gn_ref[...]), 0.0)

    base = HALO - ksize // 2
    span = sh_sc.shape[1]
    for c0 in range(0, cw, lanes):
        wc = w_ref[:, c0:c0 + lanes]
        bc = b_ref[:, c0:c0 + lanes]
        for s in range(SUBLANES):
            sh_sc[s] = y_sc[s:s + span, c0:c0 + lanes]
        for r0 in range(0, tt, rows):
            acc = jnp.broadcast_to(bc, (rows, lanes))
            for j in range(ksize):
                s = (base + j) % SUBLANES
                top = r0 + base + j - s
                acc = acc + wc[j:j + 1, :] * sh_sc[s, top:top + rows, :]
            c_sc[r0:r0 + rows, c0:c0 + lanes] = acc

    y = c_sc[...]
    mu = jnp.mean(y, axis=-1, keepdims=True)
    yc = y - mu
    var = jnp.mean(yc * yc, axis=-1, keepdims=True)
    z = yc * lax.rsqrt(var + EPS) * lg_ref[...] + lb_ref[...]
    o_ref[...] = (z * jax.nn.sigmoid(z)).astype(o_ref.dtype)


def conformer_conv_pre(u, dw_w, dw_b, ln_g, ln_b):
    t, c2 = u.shape
    cw = c2 // 2
    ksize = dw_w.shape[0]
    assert ksize // 2 <= HALO - 1
    tt = _tile(t, 256, HALO)
    hb = tt // HALO
    n_hb = t // HALO
    cur_a = pl.BlockSpec((tt, cw), lambda i: (i, 0))
    cur_g = pl.BlockSpec((tt, cw), lambda i: (i, 1))
    prev_a = pl.BlockSpec((HALO, cw), lambda i: (jnp.maximum(i * hb - 1, 0), 0))
    prev_g = pl.BlockSpec((HALO, cw), lambda i: (jnp.maximum(i * hb - 1, 0), 1))
    next_a = pl.BlockSpec((HALO, cw), lambda i: (jnp.minimum((i + 1) * hb, n_hb - 1), 0))
    next_g = pl.BlockSpec((HALO, cw), lambda i: (jnp.minimum((i + 1) * hb, n_hb - 1), 1))
    vec = pl.BlockSpec((1, cw), lambda i: (0, 0))
    lanes = min(512, cw)
    kern = functools.partial(_conv_kernel, ksize=ksize, rows=min(64, tt), lanes=lanes)
    return pl.pallas_call(
        kern,
        grid=(t // tt,),
        in_specs=[cur_a, cur_g, prev_a, prev_g, next_a, next_g,
                  pl.BlockSpec((ksize, cw), lambda i: (0, 0)), vec, vec, vec],
        out_specs=pl.BlockSpec((tt, cw), lambda i: (i, 0)),
        out_shape=jax.ShapeDtypeStruct((t, cw), BF16),
        scratch_shapes=[pltpu.VMEM((tt + 2 * HALO, cw), F32), pltpu.VMEM((tt, cw), F32),
                        pltpu.VMEM((SUBLANES, tt + 2 * HALO - SUBLANES, lanes), F32)],
        compiler_params=_params(("parallel",), 40),
        name="conformer_conv",
    )(u, u, u, u, u, u, dw_w, dw_b, ln_g, ln_b)


def _merge_kernel(a_ref, wa_ref, b_ref, wb_ref, ga_ref, gb_ref, o_ref):
    pa = jnp.dot(a_ref[...], wa_ref[...], preferred_element_type=F32)
    pb = jnp.dot(b_ref[...], wb_ref[...], preferred_element_type=F32)
    ga = jax.nn.sigmoid(ga_ref[...].astype(F32))
    gb = jax.nn.sigmoid(gb_ref[...].astype(F32))
    o_ref[...] = (ga * pa + gb * pb).astype(o_ref.dtype)


def gated_merge(attn, w_attn_o, conv, w_conv_o, gates):
    t, ka = attn.shape
    kb = conv.shape[1]
    d = w_attn_o.shape[1]
    tm = _tile(t, 512, 8)
    tn = _tile(d, 1024)
    nb = d // tn
    return pl.pallas_call(
        _merge_kernel,
        grid=(nb, t // tm),
        in_specs=[
            pl.BlockSpec((tm, ka), lambda j, i: (i, 0)),
            pl.BlockSpec((ka, tn), lambda j, i: (0, j)),
            pl.BlockSpec((tm, kb), lambda j, i: (i, 0)),
            pl.BlockSpec((kb, tn), lambda j, i: (0, j)),
            pl.BlockSpec((tm, tn), lambda j, i: (i, j)),
            pl.BlockSpec((tm, tn), lambda j, i: (i, nb + j)),
        ],
        out_specs=pl.BlockSpec((tm, tn), lambda j, i: (i, j)),
        out_shape=jax.ShapeDtypeStruct((t, d), BF16),
        compiler_params=_params(("parallel", "parallel"), 48),
        name="gated_merge",
    )(attn, w_attn_o, conv, w_conv_o, gates, gates)


def _mm_resid_kernel(a_ref, w_ref, x_ref, g_ref, o_ref):
    acc = jnp.dot(a_ref[...], w_ref[...], preferred_element_type=F32)
    o_ref[...] = x_ref[...] + g_ref[...] * acc


def matmul_gated_residual(a, w, x, gate):
    m, k = a.shape
    n = w.shape[1]
    tm = _tile(m, 1024, 8)
    tn = _tile(n, 1024)
    return pl.pallas_call(
        _mm_resid_kernel,
        grid=(n // tn, m // tm),
        in_specs=[
            pl.BlockSpec((tm, k), lambda j, i: (i, 0)),
            pl.BlockSpec((k, tn), lambda j, i: (0, j)),
            pl.BlockSpec((tm, tn), lambda j, i: (i, j)),
            pl.BlockSpec((1, tn), lambda j, i: (0, j)),
        ],
        out_specs=pl.BlockSpec((tm, tn), lambda j, i: (i, j)),
        out_shape=jax.ShapeDtypeStruct((m, n), F32),
        compiler_params=_params(("parallel", "parallel"), 56),
        name="out_proj_residual",
    )(a, w, x, gate)


HI_HALF_MASK = -65536


def _pack_bf16_pair(lo, hi):
    lo_b = lax.bitcast_convert_type(lo.astype(BF16).astype(F32), jnp.int32)
    hi_b = lax.bitcast_convert_type(hi.astype(BF16).astype(F32), jnp.int32)
    return (hi_b & HI_HALF_MASK) | lax.shift_right_logical(lo_b, jnp.full(lo_b.shape, 16, jnp.int32))


def _unpack_bf16_pair(w):
    lo = lax.bitcast_convert_type(w << 16, F32)
    hi = lax.bitcast_convert_type(w & HI_HALF_MASK, F32)
    return lo, hi


def _router_kernel(x_ref, g_ref, sc_ref, sh_ref, wh_ref, wl_ref, h_ref, aff_ref, *, n_experts):
    x = x_ref[...]
    half = x.shape[1] // 2
    ms = jnp.mean(x * x, axis=-1, keepdims=True)
    h = x * lax.rsqrt(ms + EPS) * g_ref[...]
    h = h * (1.0 + sc_ref[...]) + sh_ref[...]
    hb = h.astype(BF16)
    h_ref[...] = _pack_bf16_pair(h[:, :half], h[:, half:])
    hl = (h - hb.astype(F32)).astype(BF16)
    logits = (jnp.dot(hb, wh_ref[...], preferred_element_type=F32)
              + jnp.dot(hb, wl_ref[...], preferred_element_type=F32)
              + jnp.dot(hl, wh_ref[...], preferred_element_type=F32))
    lane = lax.broadcasted_iota(jnp.int32, logits.shape, 1)
    logits = jnp.where(lane < n_experts, logits, -jnp.inf)
    mx = jnp.max(logits, axis=-1, keepdims=True)
    e = jnp.exp(logits - mx)
    aff_ref[...] = e / jnp.sum(e, axis=-1, keepdims=True)


def norm_router(x, g, scale, shift, w_router):
    t, d = x.shape
    n_experts = w_router.shape[1]
    wp = jnp.zeros((d, LANE), F32).at[:, :n_experts].set(w_router)
    wh = wp.astype(BF16)
    wl = (wp - wh.astype(F32)).astype(BF16)
    tm = _tile(t, 256, 8)
    vec = pl.BlockSpec((1, d), lambda i: (0, 0))
    wspec = pl.BlockSpec((d, LANE), lambda i: (0, 0))
    kern = functools.partial(_router_kernel, n_experts=n_experts)
    return pl.pallas_call(
        kern,
        grid=(t // tm,),
        in_specs=[pl.BlockSpec((tm, d), lambda i: (i, 0)), vec, vec, vec, wspec, wspec],
        out_specs=[pl.BlockSpec((tm, d // 2), lambda i: (i, 0)), pl.BlockSpec((tm, LANE), lambda i: (i, 0))],
        out_shape=[jax.ShapeDtypeStruct((t, d // 2), jnp.int32), jax.ShapeDtypeStruct((t, LANE), F32)],
        compiler_params=_params(("parallel",), 40),
        name="norm_router",
    )(x, g, scale, shift, wh, wl)


def _block_cumsum(m):
    nb = m.shape[0]
    r = lax.broadcasted_iota(jnp.int32, (LANE, LANE), 0)
    c = lax.broadcasted_iota(jnp.int32, (LANE, LANE), 1)
    upper = jnp.where(r <= c, 1.0, 0.0).astype(BF16)
    inblock = jnp.dot(m.astype(BF16), upper, preferred_element_type=F32)
    tot = jnp.broadcast_to(inblock[:, LANE - 1:LANE], (nb, LANE))
    rb = lax.broadcasted_iota(jnp.int32, (nb, nb), 0)
    cb = lax.broadcasted_iota(jnp.int32, (nb, nb), 1)
    lower = jnp.where(cb < rb, 1.0, 0.0).astype(BF16)
    before = jnp.dot(lower, tot.astype(BF16), preferred_element_type=F32)
    return inblock, before


def _select_kernel(a_ref, idx_ref, val_ref, soff_ref, *, cap):
    a = a_ref[0]
    nb = a.shape[0]
    bits = lax.bitcast_convert_type(a, jnp.int32)
    capf = jnp.float32(cap)

    def count(pred):
        cnt = jnp.sum(jnp.where(pred, 1.0, 0.0), axis=0, keepdims=True)
        return jnp.sum(cnt, axis=1, keepdims=True)

    def bisect(i, thr):
        cand = thr | lax.shift_left(jnp.int32(1), 30 - i)
        return jnp.where(count(bits >= cand) >= capf, cand, thr)

    thr = lax.fori_loop(0, 31, bisect, jnp.zeros((1, 1), jnp.int32))
    gt = bits > thr
    eq = bits == thr
    need = capf - count(gt)
    eqf = jnp.where(eq, 1.0, 0.0)
    eq_in, eq_before = _block_cumsum(eqf)
    take_eq = eq & ((eq_in + eq_before - eqf) < need)
    sel = jnp.where(gt | take_eq, 1.0, 0.0)
    inblock, before = _block_cumsum(sel)

    slot = lax.broadcasted_iota(jnp.int32, (1, cap), 1).astype(F32)
    upto = before[:, 0:1] + inblock[:, LANE - 1:LANE]
    blk = jnp.sum(jnp.where(upto <= slot, 1.0, 0.0), axis=0, keepdims=True)
    bi = lax.broadcasted_iota(jnp.int32, (nb, cap), 0).astype(F32)
    onehot = jnp.where(bi == blk, 1.0, 0.0).astype(BF16)

    rows = jnp.dot(inblock.T.astype(BF16), onehot, preferred_element_type=F32)
    before_t = before.T
    hi = jnp.floor(before_t * (1.0 / LANE))
    lo = before_t - hi * LANE
    start = (jnp.dot(hi.astype(BF16), onehot, preferred_element_type=F32) * LANE
             + jnp.dot(lo.astype(BF16), onehot, preferred_element_type=F32))
    rank = slot - start
    lane_of = jnp.sum(jnp.where(rows <= rank, 1.0, 0.0), axis=0, keepdims=True)
    idx_ref[0] = (blk * LANE + lane_of).astype(jnp.int32)

    at = a.T
    a1 = at.astype(BF16)
    r1 = at - a1.astype(F32)
    a2 = r1.astype(BF16)
    a3 = (r1 - a2.astype(F32)).astype(BF16)
    picked = (jnp.dot(a1, onehot, preferred_element_type=F32)
              + jnp.dot(a2, onehot, preferred_element_type=F32)
              + jnp.dot(a3, onehot, preferred_element_type=F32))
    li = lax.broadcasted_iota(jnp.int32, (LANE, cap), 0).astype(F32)
    val_ref[0] = jnp.sum(jnp.where(li == lane_of, picked, 0.0), axis=0, keepdims=True)
    soff_ref[0] = before_t[0:1, :].astype(jnp.int32)


def expert_select(aff3, cap):
    e, nb, _ = aff3.shape
    kern = functools.partial(_select_kernel, cap=cap)
    return pl.pallas_call(
        kern,
        grid=(e,),
        in_specs=[pl.BlockSpec((1, nb, LANE), lambda i: (i, 0, 0))],
        out_specs=[pl.BlockSpec((1, 1, cap), lambda i: (i, 0, 0)),
                   pl.BlockSpec((1, 1, cap), lambda i: (i, 0, 0)),
                   pl.BlockSpec((1, 1, nb), lambda i: (i, 0, 0))],
        out_shape=[jax.ShapeDtypeStruct((e, 1, cap), jnp.int32),
                   jax.ShapeDtypeStruct((e, 1, cap), F32),
                   jax.ShapeDtypeStruct((e, 1, nb), jnp.int32)],
        compiler_params=_params(("parallel",), 32),
        name="expert_select",
    )(aff3)


def _expert_up_kernel(idx_ref, h_hbm, wg_ref, wu_ref, o_ref, stage, x_sc, sem, *, cap, phase_rows):
    e = pl.program_id(0)
    j = pl.program_id(1)
    half = x_sc.shape[1] // 2
    n_phases = cap // phase_rows
    unroll = SUBLANES
    assert phase_rows % unroll == 0

    def row_copy(p, r):
        tok = idx_ref[e * cap + p * phase_rows + r]
        return pltpu.make_async_copy(h_hbm.at[pl.ds(tok, 1), :],
                                     stage.at[p % 2, pl.ds(r, 1), :], sem.at[p % 2])

    def start_phase(p):
        def go(r, carry):
            for k in range(unroll):
                row_copy(p, r * unroll + k).start(priority=k % 2)
            return carry
        lax.fori_loop(0, phase_rows // unroll, go, 0)

    def wait_phase(p):
        def go(r, carry):
            for k in range(unroll):
                row_copy(p, r * unroll + k).wait()
            return carry
        lax.fori_loop(0, phase_rows // unroll, go, 0)

    @pl.when(j == 0)
    def _():
        start_phase(0)
        for p in range(n_phases):
            if p + 1 < n_phases:
                start_phase(p + 1)
            wait_phase(p)
            lo, hi = _unpack_bf16_pair(stage[p % 2])
            x_sc[p * phase_rows:(p + 1) * phase_rows, :half] = lo.astype(BF16)
            x_sc[p * phase_rows:(p + 1) * phase_rows, half:] = hi.astype(BF16)

    tf = o_ref.shape[2]
    w = jnp.concatenate([wg_ref[0].astype(BF16), wu_ref[0].astype(BF16)], axis=1)
    gu = jnp.dot(x_sc[...], w, preferred_element_type=F32)
    g, u = gu[:, :tf], gu[:, tf:]
    o_ref[0] = (g * jax.nn.sigmoid(g) * u).astype(o_ref.dtype)


def expert_up(idx_flat, h_packed, w_gate, w_up, cap):
    e, d, f = w_gate.shape
    half = d // 2
    tf = _tile(f, 256)
    phase_rows = _tile(cap, 512, 8)
    wspec = pl.BlockSpec((1, d, tf), lambda i, j, idx: (i, 0, j))
    kern = functools.partial(_expert_up_kernel, cap=cap, phase_rows=phase_rows)
    return pl.pallas_call(
        kern,
        grid_spec=pltpu.PrefetchScalarGridSpec(
            num_scalar_prefetch=1,
            grid=(e, f // tf),
            in_specs=[pl.BlockSpec(memory_space=pl.ANY), wspec, wspec],
            out_specs=pl.BlockSpec((1, cap, tf), lambda i, j, idx: (i, 0, j)),
            scratch_shapes=[pltpu.VMEM((2, phase_rows, half), jnp.int32),
                            pltpu.VMEM((cap, d), BF16),
                            pltpu.SemaphoreType.DMA((2,))],
        ),
        out_shape=jax.ShapeDtypeStruct((e, cap, f), BF16),
        compiler_params=_params(("arbitrary", "arbitrary"), 56),
        name="expert_up_gather",
    )(idx_flat, h_packed, w_gate, w_up)


def _expert_down_kernel(h_ref, wa_ref, wb_ref, v_ref, o_ref):
    h = h_ref[0]
    v = v_ref[0]
    tn = o_ref.shape[2]
    w = jnp.concatenate([wa_ref[0].astype(BF16), wb_ref[0].astype(BF16)], axis=1)
    y = jnp.dot(h, w, preferred_element_type=F32) * v
    o_ref[0] = _pack_bf16_pair(y[:, :tn], y[:, tn:])


def expert_down(hid, w_down, vals):
    e, c, f = hid.shape
    d = w_down.shape[2]
    half = d // 2
    tn = _tile(half, 256)
    nb = half // tn
    return pl.pallas_call(
        _expert_down_kernel,
        grid=(e, nb),
        in_specs=[pl.BlockSpec((1, c, f), lambda i, j: (i, 0, 0)),
                  pl.BlockSpec((1, f, tn), lambda i, j: (i, 0, j)),
                  pl.BlockSpec((1, f, tn), lambda i, j: (i, 0, nb + j)),
                  pl.BlockSpec((1, c, 1), lambda i, j: (i, 0, 0))],
        out_specs=pl.BlockSpec((1, c, tn), lambda i, j: (i, 0, j)),
        out_shape=jax.ShapeDtypeStruct((e, c, half), jnp.int32),
        compiler_params=_params(("parallel", "arbitrary"), 48),
        name="expert_down",
    )(hid, w_down, w_down, vals)


def _combine_kernel(idx_ref, soff_ref, x_ref, y_hbm, g2_ref, fg_ref, o_ref, stage, sem,
                    *, cap, n_blocks, n_experts):
    b = pl.program_id(0)
    tt, d = x_ref.shape
    half = d // 2
    n_groups = sem.shape[0]
    per_group = n_experts // n_groups

    def slot_range(e):
        lo = soff_ref[e * n_blocks + b]
        nxt = soff_ref[e * n_blocks + jnp.minimum(b + 1, n_blocks - 1)]
        return lo, jnp.where(b + 1 < n_blocks, nxt, cap)

    def row_copy(e, j):
        tok = idx_ref[e * cap + j]
        return pltpu.make_async_copy(y_hbm.at[e, pl.ds(j, 1), :],
                                     stage.at[e, pl.ds(tok - b * tt, 1), :], sem.at[e // per_group])

    for grp in range(n_groups):
        first = grp * per_group
        stage[first:first + per_group] = jnp.zeros((per_group,) + stage.shape[1:], stage.dtype)
        for e in range(first, first + per_group):
            lo, hi = slot_range(e)

            def go(j, carry, e=e):
                row_copy(e, j).start()
                return carry
            lax.fori_loop(lo, hi, go, 0)

    for e in range(n_experts):
        lo, hi = slot_range(e)

        def done(j, carry, e=e):
            row_copy(e, j).wait()
            return carry
        lax.fori_loop(lo, hi, done, 0)

    f_lo = jnp.zeros((tt, half), F32)
    f_hi = jnp.zeros((tt, half), F32)
    for e in range(n_experts):
        lo, hi = _unpack_bf16_pair(stage[e])
        f_lo = f_lo + lo
        f_hi = f_hi + hi
    x = x_ref[...]
    x_lo = x[:, :half] + g2_ref[:, :half] * f_lo
    x_hi = x[:, half:] + g2_ref[:, half:] * f_hi
    ms = (jnp.sum(x_lo * x_lo, axis=-1, keepdims=True)
          + jnp.sum(x_hi * x_hi, axis=-1, keepdims=True)) * (1.0 / d)
    inv = lax.rsqrt(ms + EPS)
    o_ref[:, :half] = x_lo * inv * fg_ref[:, :half]
    o_ref[:, half:] = x_hi * inv * fg_ref[:, half:]


COMBINE_GROUPS = 4


def combine_final_norm(idx_flat, soff_flat, x, y_packed, gate, final_g, cap):
    t, d = x.shape
    n_experts = y_packed.shape[0]
    tt = LANE
    n_blocks = t // tt
    vec = pl.BlockSpec((1, d), lambda i, a, b: (0, 0))
    row = pl.BlockSpec((tt, d), lambda i, a, b: (i, 0))
    kern = functools.partial(_combine_kernel, cap=cap, n_blocks=n_blocks, n_experts=n_experts)
    return pl.pallas_call(
        kern,
        grid_spec=pltpu.PrefetchScalarGridSpec(
            num_scalar_prefetch=2,
            grid=(n_blocks,),
            in_specs=[row, pl.BlockSpec(memory_space=pl.ANY), vec, vec],
            out_specs=row,
            scratch_shapes=[pltpu.VMEM((n_experts, tt, d // 2), jnp.int32),
                            pltpu.SemaphoreType.DMA((COMBINE_GROUPS if n_experts % COMBINE_GROUPS == 0 else 1,))],
        ),
        out_shape=jax.ShapeDtypeStruct((t, d), F32),
        compiler_params=_params(("arbitrary",), 48),
        name="combine_final_norm",
    )(idx_flat, soff_flat, x, y_packed, gate, final_g)


def kernel(x, c, ctx, c_ctx, w_ada, b_ada, norm1_g, w_in, b_in, q_norm_g, k_norm_g, w_attn_o,
           dw_w, dw_b, conv_ln_g, conv_ln_b, w_conv_o, w_out, norm2_g, w_router, w_gate, w_up,
           w_down, final_g):
    batch, t, d = x.shape
    depth = w_ada.shape[0]
    assert batch == 1 and depth == 1
    q_w = w_attn_o.shape[1]
    conv_w = w_conv_o.shape[1]
    in_w = w_in.shape[2]
    kv_w = (in_w - q_w - 2 * conv_w - 2 * d) // 2
    n_kv = kv_w // HEAD_DIM
    n_experts = w_router.shape[2]
    cap = CAPACITY_FACTOR * t // n_experts
    o1, o2, o3, o4 = q_w, q_w + kv_w, q_w + 2 * kv_w, q_w + 2 * kv_w + 2 * conv_w

    row = lambda v: v.reshape(1, -1)
    x2, ctx2 = x[0], ctx[0]

    cvecs_t = jnp.stack([c[0], c_ctx], axis=1)
    mod = ada_modulation(cvecs_t, w_ada[0], row(b_ada[0]))
    sh1, sc1, g1, sh2, sc2, g2 = [mod[0:1, i * d:(i + 1) * d] for i in range(N_MOD)]
    csh1, csc1 = mod[1:2, 0:d], mod[1:2, d:2 * d]

    h_lat = norm_modulate(x2, row(norm1_g[0]), sc1, sh1)
    h_ctx = norm_modulate(ctx2, row(norm1_g[0]), csc1, csh1)

    w_in0, b_in0 = w_in[0], row(b_in[0])
    cosf, sins = rope_tables(t)
    qg, kg = row(q_norm_g[0]), row(k_norm_g[0])
    gw = q_w // n_kv
    assert gw == kv_w
    q_scale = HEAD_DIM ** -0.5 * LOG2E
    qkv = qkv_projection(h_lat, w_in0, b_in0, qg, kg, cosf, sins,
                         block_w=gw, n_q_blocks=n_kv, q_scale=q_scale)
    u = matmul_bias(h_lat, w_in0, b_in0, o3, o4 - o3)
    gates = matmul_bias(h_lat, w_in0, b_in0, o4, in_w - o4)
    kv_ctx = matmul_bias(h_ctx, w_in0, b_in0, o1, o3 - o1)

    tc = ctx2.shape[0]
    k_ctx = head_norm(kv_ctx, 0, 1, kv_w, kg, cosf[:tc], sins[:tc], rope=False, scale=1.0)
    k_all = jnp.concatenate([k_ctx, qkv[:, o1:o2]], axis=0)
    v_all = jnp.concatenate([kv_ctx[:, kv_w:], qkv[:, o2:o3]], axis=0)
    score_bound = (1.02 * HEAD_DIM * q_scale) * jnp.max(jnp.abs(q_norm_g[0])) * jnp.max(jnp.abs(k_norm_g[0]))
    attn = gqa_attention(qkv, q_w, k_all, v_all, n_kv, score_bound)

    conv = conformer_conv_pre(u, dw_w[0], row(dw_b[0]), row(conv_ln_g[0]), row(conv_ln_b[0]))
    merged = gated_merge(attn, w_attn_o[0].astype(BF16), conv, w_conv_o[0].astype(BF16), gates)
    x1 = matmul_gated_residual(merged, w_out[0].astype(BF16), x2, g1)

    h2p, aff = norm_router(x1, row(norm2_g[0]), sc2, sh2, w_router[0])
    aff3 = aff[:, :n_experts].T.reshape(n_experts, t // LANE, LANE)
    idx, vals, soff = expert_select(aff3, cap)
    idx_flat = idx.reshape(-1)
    hid = expert_up(idx_flat, h2p, w_gate[0], w_up[0], cap)
    y = expert_down(hid, w_down[0], vals.reshape(n_experts, cap, 1))
    out = combine_final_norm(idx_flat, soff.reshape(-1), x1, y, g2, row(final_g), cap)
    return out[None]
```
